```python
import math
import jax
import jax.numpy as jnp
from jax import lax

D_MODEL = 4096
BATCH = 4
SEQ = 2048
DEPTH = 2
DEC_BATCH = 8
DEC_SEQ = 8
PAST_LEN = 16384
PAGE_SIZE = 128

RW_WIDTH = D_MODEL // 4
CV_WIDTH = D_MODEL // 4
ATT_WIDTH = D_MODEL - RW_WIDTH - CV_WIDTH
RW_HEAD_DIM = 64
RW_HEADS = RW_WIDTH // RW_HEAD_DIM
RW_DECAY_LORA = 64
RW_ICLR_LORA = 64
RW_GATE_LORA = 160
RW_COLS = 3 * RW_WIDTH + RW_DECAY_LORA + RW_ICLR_LORA + RW_GATE_LORA
RW_LNX_EPS = 64e-5
CONV_K = 31
ATT_HEAD_DIM = 128
ATT_HEADS = ATT_WIDTH // ATT_HEAD_DIM
MOBA_BLOCK = 256
MOBA_TOPK = 3
MOBA_Q_CHUNK = 16
PROJ_COLS = RW_COLS + 2 * CV_WIDTH + 3 * ATT_WIDTH
N_EXPERTS = 32
N_EXPERT_GROUPS = 4
EXPERTS_PER_GROUP = N_EXPERTS // N_EXPERT_GROUPS
TOP_K = 2
EXPERT_FF = 1024
MOE_BLOCK = 128
DEEPNORM_ALPHA = (2 * DEPTH) ** 0.25
DEEPNORM_BETA = (8 * DEPTH) ** -0.25
LN_EPS = 1e-5

kernel_name = 'hybrid_rwkv7_conformer_moba_moe_decode_step'


def layer_norm(x, g, b, eps=LN_EPS):
    xf = x.astype(jnp.float32)
    mu = jnp.mean(xf, axis=-1, keepdims=True)
    var = jnp.mean(jnp.square(xf - mu), axis=-1, keepdims=True)
    return ((xf - mu) * lax.rsqrt(var + eps) * g + b).astype(x.dtype)


def rwkv7_group(p, shift0, wkv0, mix, w0, w2, a0, a2, g2, k_k, k_a, r_k, lnx_g, lnx_b):
    B, T, _ = p.shape
    f32 = jnp.float32
    prev = jnp.concatenate([shift0[:, None, :], p[:, :-1]], axis=1)
    xm = (p + (prev - p) * mix).astype(f32)
    cuts = [RW_WIDTH, 2 * RW_WIDTH, 3 * RW_WIDTH, 3 * RW_WIDTH + RW_DECAY_LORA,
            3 * RW_WIDTH + RW_DECAY_LORA + RW_ICLR_LORA]
    r, k, v, xw, xa, xg = jnp.split(xm, cuts, axis=-1)
    w_log = -jax.nn.softplus(-(w0 + jnp.tanh(xw) @ w2)) - 0.5
    decay = jnp.exp(-jnp.exp(w_log))
    a = jax.nn.sigmoid(a0 + xa @ a2)
    g = jax.nn.sigmoid(xg) @ g2
    hs = lambda t: t.reshape(B, T, RW_HEADS, RW_HEAD_DIM)
    r, k, v, decay, a = hs(r), hs(k), hs(v), hs(decay), hs(a)
    kk = k * k_k.reshape(RW_HEADS, RW_HEAD_DIM)
    kk = kk / jnp.maximum(jnp.sqrt(jnp.sum(kk * kk, axis=-1, keepdims=True)), 1e-12)
    k = k * (1.0 + (a - 1.0) * k_a.reshape(RW_HEADS, RW_HEAD_DIM))

    def step(S, xs):
        r_t, w_t, k_t, v_t, kk_t, a_t = xs
        sk = jnp.einsum('bhvk,bhk->bhv', S, kk_t)
        S = (S * w_t[:, :, None, :] - sk[..., None] * (kk_t * a_t)[:, :, None, :]
             + v_t[..., None] * k_t[:, :, None, :])
        return S, jnp.einsum('bhvk,bhk->bhv', S, r_t)

    xs = tuple(jnp.swapaxes(t, 0, 1) for t in (r, decay, k, v, kk, a))
    S, o = lax.scan(step, wkv0.astype(f32), xs)
    o = jnp.swapaxes(o, 0, 1)
    mu = jnp.mean(o, axis=-1, keepdims=True)
    var = jnp.mean(jnp.square(o - mu), axis=-1, keepdims=True)
    o = ((o - mu) * lax.rsqrt(var + RW_LNX_EPS)).reshape(B, T, RW_WIDTH) * lnx_g + lnx_b
    bonus = jnp.sum(r * k * r_k, axis=-1, keepdims=True) * v
    o = (o + bonus.reshape(B, T, RW_WIDTH)) * g
    return o.astype(p.dtype), S.astype(wkv0.dtype), p[:, -1]


def conformer_conv_group(p, conv0, dw_w, dw_b, ln_g, ln_b):
    val, gate = jnp.split(p, 2, axis=-1)
    u = val * jax.nn.sigmoid(gate)
    u_ext = jnp.concatenate([conv0, u], axis=1)
    y = lax.conv_general_dilated(u_ext, dw_w[:, None, :], window_strides=(1,), padding='VALID',
                                 dimension_numbers=('NWC', 'WIO', 'NWC'),
                                 feature_group_count=CV_WIDTH) + dw_b
    y = jax.nn.silu(layer_norm(y, ln_g, ln_b))
    return y, u_ext[:, -(CONV_K - 1):]


def moba_attention(q, k, v, q_pos):
    B, H, T, hd = q.shape
    L = k.shape[2]
    n_blk = max(-(-L // MOBA_BLOCK), MOBA_TOPK)
    pad = ((0, 0), (0, 0), (0, n_blk * MOBA_BLOCK - L), (0, 0))
    kb = jnp.pad(k, pad).reshape(B, H, n_blk, MOBA_BLOCK, hd)
    vb = jnp.pad(v, pad).reshape(B, H, n_blk, MOBA_BLOCK, hd)
    k_mean = jnp.mean(kb.astype(jnp.float32), axis=3)
    qc = math.gcd(T, MOBA_Q_CHUNK)
    nc = T // qc
    q_chunks = q.reshape(B, H, nc, qc, hd).transpose(2, 0, 1, 3, 4)
    pos_chunks = q_pos.reshape(nc, qc)
    b_ix = jnp.arange(B)[:, None, None, None]
    h_ix = jnp.arange(H)[None, :, None, None]
    blk_ix = jnp.arange(n_blk)
    off = jnp.arange(MOBA_BLOCK)
    scale = ATT_HEAD_DIM ** -0.5

    def attend_chunk(args):
        qb, pb = args
        cur = pb // MOBA_BLOCK
        gate = jnp.einsum('bhqd,bhnd->bhqn', qb.astype(jnp.float32), k_mean)
        gate = jnp.where(blk_ix[None, :] < cur[:, None], gate, -jnp.inf)
        _, top = lax.top_k(gate, MOBA_TOPK)
        top_ok = top < cur[:, None]
        top = jnp.where(top_ok, top, 0)
        own = jnp.broadcast_to(cur[None, None, :, None], (B, H, qc, 1))
        sel = jnp.concatenate([top, own], axis=-1)
        kg = kb[b_ix, h_ix, sel]
        vg = vb[b_ix, h_ix, sel]
        s = jnp.einsum('bhqd,bhqjsd->bhqjs', qb, kg).astype(jnp.float32) * scale
        own_ok = (cur[:, None] * MOBA_BLOCK + off[None, :]) <= pb[:, None]
        mask = jnp.concatenate(
            [jnp.broadcast_to(top_ok[..., None], (B, H, qc, MOBA_TOPK, MOBA_BLOCK)),
             jnp.broadcast_to(own_ok[None, None, :, None, :], (B, H, qc, 1, MOBA_BLOCK))], axis=3)
        s = jnp.where(mask, s, -jnp.inf).reshape(B, H, qc, (MOBA_TOPK + 1) * MOBA_BLOCK)
        pr = jax.nn.softmax(s, axis=-1).reshape(B, H, qc, MOBA_TOPK + 1, MOBA_BLOCK)
        return jnp.einsum('bhqjs,bhqjsd->bhqd', pr.astype(vg.dtype), vg)

    out = lax.map(attend_chunk, (q_chunks, pos_chunks))
    return out.transpose(1, 2, 0, 3, 4).reshape(B, H, T, hd)


def moba_group(p, k_past, v_past, pos0):
    B, T, _ = p.shape
    q, k, v = [t.reshape(B, T, ATT_HEADS, ATT_HEAD_DIM) for t in jnp.split(p, 3, axis=-1)]
    kf = jnp.concatenate([k_past, k], axis=1).transpose(0, 2, 1, 3)
    vf = jnp.concatenate([v_past, v], axis=1).transpose(0, 2, 1, 3)
    q_pos = pos0 + jnp.arange(T, dtype=jnp.int32)
    o = moba_attention(q.transpose(0, 2, 1, 3), kf, vf, q_pos)
    return o.transpose(0, 2, 1, 3).reshape(B, T, ATT_WIDTH), k, v


def moe_ffn(h, router_w, router_b, w1, w3, w2):
    n_tok, d = h.shape
    probs = jax.nn.softmax((h @ router_w).astype(jnp.float32), axis=-1)
    sel = (probs + router_b).reshape(n_tok, N_EXPERT_GROUPS, EXPERTS_PER_GROUP)
    grp_score = jnp.sum(lax.top_k(sel, TOP_K)[0], axis=-1)
    best = jnp.argmax(grp_score, axis=-1)
    in_grp = jnp.take_along_axis(sel, best[:, None, None], axis=1)[:, 0]
    _, local = lax.top_k(in_grp, TOP_K)
    eidx = best[:, None] * EXPERTS_PER_GROUP + local
    gates = jnp.take_along_axis(probs, eidx, axis=1)
    gates = gates / jnp.sum(gates, axis=-1, keepdims=True)
    n_asg = n_tok * TOP_K
    flat_e = eidx.reshape(n_asg)
    flat_tok = jnp.arange(n_asg, dtype=jnp.int32) // TOP_K
    flat_g = gates.reshape(n_asg)
    order = jnp.argsort(flat_e)
    se, stok, sg = flat_e[order], flat_tok[order], flat_g[order]
    counts = jnp.bincount(flat_e, length=N_EXPERTS)
    starts = jnp.cumsum(counts) - counts
    pcounts = (counts + MOE_BLOCK - 1) // MOE_BLOCK * MOE_BLOCK
    pends = jnp.cumsum(pcounts)
    pstarts = pends - pcounts
    dest = pstarts[se] + jnp.arange(n_asg) - starts[se]
    n_blk = -(-(n_asg + N_EXPERTS * (MOE_BLOCK - 1)) // MOE_BLOCK)
    cap = n_blk * MOE_BLOCK
    buf_tok = jnp.zeros((cap,), jnp.int32).at[dest].set(stok)
    buf_ok = jnp.zeros((cap,), bool).at[dest].set(True)
    xb = jnp.where(buf_ok[:, None], h[buf_tok], 0).reshape(n_blk, MOE_BLOCK, d)
    blk_start = jnp.arange(n_blk) * MOE_BLOCK
    blk_e = jnp.minimum(jnp.sum(pends[None, :] <= blk_start[:, None], axis=1), N_EXPERTS - 1)

    def expert_block(args):
        xk, e = args
        return (jax.nn.silu(xk @ w1[e]) * (xk @ w3[e])) @ w2[e]

    yb = lax.map(expert_block, (xb, blk_e)).reshape(cap, d)
    return jnp.zeros((n_tok, d), h.dtype).at[stok].add(yb[dest] * sg[:, None].astype(h.dtype))


def trunk_layer(x, c, shift0, wkv0, conv0, k_past, v_past, pos0, lp, router_w, router_b):
    B, T, D = x.shape
    ada = jax.nn.silu(c) @ lp['w_ada'] + lp['b_ada']
    sh1, sc1, gt1, sh2, sc2, gt2 = jnp.split(ada[:, None, :], 6, axis=-1)
    h = x * (1.0 + sc1) + sh1
    proj = h @ lp['w_in']
    p_rw = proj[..., :RW_COLS]
    p_cv = proj[..., RW_COLS:RW_COLS + 2 * CV_WIDTH]
    p_at = proj[..., RW_COLS + 2 * CV_WIDTH:]
    o_rw, wkv_new, shift_new = rwkv7_group(p_rw, shift0, wkv0, lp['rw_mix'], lp['rw_w0'], lp['rw_w2'],
                                           lp['rw_a0'], lp['rw_a2'], lp['rw_g2'], lp['rw_kk'], lp['rw_ka'],
                                           lp['rw_rk'], lp['rw_lnx_g'], lp['rw_lnx_b'])
    o_cv, conv_new = conformer_conv_group(p_cv, conv0, lp['cv_dw'], lp['cv_db'], lp['cv_ln_g'], lp['cv_ln_b'])
    o_at, k_new, v_new = moba_group(p_at, k_past, v_past, pos0)
    mixed = jnp.concatenate([o_rw, o_cv, o_at], axis=-1) @ lp['w_out']
    x = layer_norm(DEEPNORM_ALPHA * x + gt1 * mixed, lp['ln1_g'], lp['ln1_b'])
    h2 = (x * (1.0 + sc2) + sh2).reshape(B * T, D)
    f = moe_ffn(h2, router_w, router_b, lp['ex_w1'], lp['ex_w3'], lp['ex_w2']).reshape(B, T, D)
    x = layer_norm(DEEPNORM_ALPHA * x + gt2 * f, lp['ln2_g'], lp['ln2_b'])
    return x, (k_new, v_new, wkv_new, shift_new, conv_new)


def setup_inputs(seed: int = 0) -> dict:
    key = jax.random.key(seed)
    ks = iter(jax.random.split(key, 64))
    f32 = jnp.float32

    def nrm(shape, scale):
        return jax.random.normal(next(ks), shape, f32) * scale

    def unif(shape, lo, hi):
        return jax.random.uniform(next(ks), shape, f32, lo, hi)

    n_pages = PAST_LEN // PAGE_SIZE
    n_used = DEC_BATCH * n_pages
    n_pool = n_used + max(1, n_used // 4)
    page_table = jax.random.permutation(next(ks), n_pool)[:n_used].reshape(DEC_BATCH, n_pages).astype(jnp.int32)
    d_in = D_MODEL ** -0.5
    return {
        'x_prompt': nrm((BATCH, SEQ, D_MODEL), 1.0),
        'x_sample': nrm((DEC_BATCH, DEC_SEQ, D_MODEL), 1.0),
        'cache_k': nrm((DEPTH, n_pool, PAGE_SIZE, ATT_HEADS, ATT_HEAD_DIM), 1.0),
        'cache_v': nrm((DEPTH, n_pool, PAGE_SIZE, ATT_HEADS, ATT_HEAD_DIM), 1.0),
        'page_table': page_table,
        'state_wkv': nrm((DEPTH, DEC_BATCH, RW_HEADS, RW_HEAD_DIM, RW_HEAD_DIM), 0.3),
        'state_shift': nrm((DEPTH, DEC_BATCH, RW_COLS), 1.0),
        'state_conv': nrm((DEPTH, DEC_BATCH, CONV_K - 1, CV_WIDTH), 0.5),
        'c_prompt': nrm((BATCH, D_MODEL), 1.0),
        'c_sample': nrm((DEC_BATCH, D_MODEL), 1.0),
        'w_ada': nrm((DEPTH, D_MODEL, 6 * D_MODEL), 0.5 * d_in),
        'b_ada': nrm((DEPTH, 6 * D_MODEL), 0.02),
        'w_in': nrm((DEPTH, D_MODEL, PROJ_COLS), d_in),
        'w_out': nrm((DEPTH, D_MODEL, D_MODEL), DEEPNORM_BETA * d_in),
        'rw_mix': unif((DEPTH, RW_COLS), 0.0, 1.0),
        'rw_w0': unif((DEPTH, RW_WIDTH), -6.0, -1.0),
        'rw_w2': nrm((DEPTH, RW_DECAY_LORA, RW_WIDTH), 0.1),
        'rw_a0': nrm((DEPTH, RW_WIDTH), 0.1),
        'rw_a2': nrm((DEPTH, RW_ICLR_LORA, RW_WIDTH), 0.1),
        'rw_g2': nrm((DEPTH, RW_GATE_LORA, RW_WIDTH), RW_GATE_LORA ** -0.5),
        'rw_kk': 0.85 + nrm((DEPTH, RW_WIDTH), 0.05),
        'rw_ka': 1.0 + nrm((DEPTH, RW_WIDTH), 0.05),
        'rw_rk': nrm((DEPTH, RW_HEADS, RW_HEAD_DIM), 0.1),
        'rw_lnx_g': 1.0 + nrm((DEPTH, RW_WIDTH), 0.05),
        'rw_lnx_b': nrm((DEPTH, RW_WIDTH), 0.02),
        'cv_dw': nrm((DEPTH, CONV_K, CV_WIDTH), CONV_K ** -0.5),
        'cv_db': nrm((DEPTH, CV_WIDTH), 0.02),
        'cv_ln_g': 1.0 + nrm((DEPTH, CV_WIDTH), 0.05),
        'cv_ln_b': nrm((DEPTH, CV_WIDTH), 0.02),
        'ln1_g': 1.0 + nrm((DEPTH, D_MODEL), 0.05),
        'ln1_b': nrm((DEPTH, D_MODEL), 0.02),
        'ln2_g': 1.0 + nrm((DEPTH, D_MODEL), 0.05),
        'ln2_b': nrm((DEPTH, D_MODEL), 0.02),
        'router_w': nrm((D_MODEL, N_EXPERTS), d_in),
        'router_b': nrm((N_EXPERTS,), 0.01),
        'ex_w1': nrm((DEPTH, N_EXPERTS, D_MODEL, EXPERT_FF), d_in),
        'ex_w3': nrm((DEPTH, N_EXPERTS, D_MODEL, EXPERT_FF), d_in),
        'ex_w2': nrm((DEPTH, N_EXPERTS, EXPERT_FF, D_MODEL), DEEPNORM_BETA * EXPERT_FF ** -0.5),
    }


def reference(x_prompt, x_sample, cache_k, cache_v, page_table, state_wkv, state_shift, state_conv,
              c_prompt, c_sample, w_ada, b_ada, w_in, w_out, rw_mix, rw_w0, rw_w2, rw_a0, rw_a2, rw_g2,
              rw_kk, rw_ka, rw_rk, rw_lnx_g, rw_lnx_b, cv_dw, cv_db, cv_ln_g, cv_ln_b,
              ln1_g, ln1_b, ln2_g, ln2_b, router_w, router_b, ex_w1, ex_w3, ex_w2):
    dt = x_prompt.dtype
    bp = x_prompt.shape[0]
    bs, n_pages = page_table.shape
    past = n_pages * cache_k.shape[2]
    shift_p0 = jnp.zeros((bp, RW_COLS), dt)
    wkv_p0 = jnp.zeros((bp, RW_HEADS, RW_HEAD_DIM, RW_HEAD_DIM), dt)
    conv_p0 = jnp.zeros((bp, CONV_K - 1, CV_WIDTH), dt)
    kv_p0 = jnp.zeros((bp, 0, ATT_HEADS, ATT_HEAD_DIM), dt)
    yp, ys = x_prompt, x_sample
    st_p, st_s = [], []
    for l in range(DEPTH):
        lp = {'w_ada': w_ada[l], 'b_ada': b_ada[l], 'w_in': w_in[l], 'w_out': w_out[l],
              'rw_mix': rw_mix[l], 'rw_w0': rw_w0[l], 'rw_w2': rw_w2[l], 'rw_a0': rw_a0[l], 'rw_a2': rw_a2[l],
              'rw_g2': rw_g2[l], 'rw_kk': rw_kk[l], 'rw_ka': rw_ka[l], 'rw_rk': rw_rk[l],
              'rw_lnx_g': rw_lnx_g[l], 'rw_lnx_b': rw_lnx_b[l],
              'cv_dw': cv_dw[l], 'cv_db': cv_db[l], 'cv_ln_g': cv_ln_g[l], 'cv_ln_b': cv_ln_b[l],
              'ln1_g': ln1_g[l], 'ln1_b': ln1_b[l], 'ln2_g': ln2_g[l], 'ln2_b': ln2_b[l],
              'ex_w1': ex_w1[l], 'ex_w3': ex_w3[l], 'ex_w2': ex_w2[l]}
        yp, sp = trunk_layer(yp, c_prompt, shift_p0, wkv_p0, conv_p0, kv_p0, kv_p0, 0, lp, router_w, router_b)
        k_past = cache_k[l, page_table].reshape(bs, past, ATT_HEADS, ATT_HEAD_DIM)
        v_past = cache_v[l, page_table].reshape(bs, past, ATT_HEADS, ATT_HEAD_DIM)
        ys, ss = trunk_layer(ys, c_sample, state_shift[l], state_wkv[l], state_conv[l], k_past, v_past, past,
                             lp, router_w, router_b)
        st_p.append(sp)
        st_s.append(ss)
    k_p, v_p, wkv_p, shift_p, conv_p = [jnp.stack([s[i] for s in st_p]) for i in range(5)]
    k_s, v_s, wkv_s, shift_s, conv_s = [jnp.stack([s[i] for s in st_s]) for i in range(5)]
    return (yp, ys, k_p, v_p, wkv_p, shift_p, conv_p, k_s, v_s, wkv_s, shift_s, conv_s)
```

```python
import functools
import math

import jax
import jax.numpy as jnp
from jax import lax
from jax.experimental import pallas as pl
from jax.experimental.pallas import tpu as pltpu

F32 = jnp.float32
BF16 = jnp.bfloat16

V7X_LANES = 128
V7X_VMEM_BYTES = 64 * 1024 * 1024
VMEM_LIMIT = V7X_VMEM_BYTES - 8 * 1024 * 1024

DEPTH = 2
RW_HEAD_DIM = 64
RW_DECAY_LORA = 64
RW_ICLR_LORA = 64
RW_GATE_LORA = 160
RW_LNX_EPS = 64e-5
CONV_K = 31
ATT_HEAD_DIM = 128
MOBA_BLOCK = 256
MOBA_TOPK = 3
N_EXPERTS = 32
N_EXPERT_GROUPS = 4
EXPERTS_PER_GROUP = N_EXPERTS // N_EXPERT_GROUPS
TOP_K = 2
DEEPNORM_ALPHA = (2 * DEPTH) ** 0.25
LN_EPS = 1e-5

RW_CHUNK = 64
CONV_HALO = 32
MOE_ROWS = 256
NEG_INF = float("-inf")


def _cparams(*sem):
    return pltpu.CompilerParams(dimension_semantics=sem, vmem_limit_bytes=VMEM_LIMIT)


def _sigmoid(x):
    return 1.0 / (1.0 + jnp.exp(-x))


def _silu(x):
    return x * _sigmoid(x)


def _split2(x):
    hi = x.astype(BF16)
    lo = (x - hi.astype(F32)).astype(BF16)
    return hi, lo


def _split3(x):
    hi = x.astype(BF16)
    r1 = x - hi.astype(F32)
    mid = r1.astype(BF16)
    lo = (r1 - mid.astype(F32)).astype(BF16)
    return hi, mid, lo


_NN = (((1,), (0,)), ((), ()))
_NT = (((1,), (1,)), ((), ()))
_TN = (((0,), (0,)), ((), ()))


def _dg(a, b, dims):
    return lax.dot_general(a, b, dims, preferred_element_type=F32)


def _dot1(a, b, dims=_NN):
    return _dg(a.astype(BF16), b.astype(BF16), dims)


def _dot3(a, b, dims=_NN):
    ah, al = _split2(a)
    bh, bl = _split2(b)
    return _dg(ah, bh, dims) + (_dg(ah, bl, dims) + _dg(al, bh, dims))


def _dot_exact_rhs(a, b_bf16, dims=_NN):
    hi, mid, lo = _split3(a)
    return _dg(hi, b_bf16, dims) + (_dg(mid, b_bf16, dims) + _dg(lo, b_bf16, dims))


def _group_ones(n, group):
    r = lax.broadcasted_iota(jnp.int32, (n, n), 0) // group
    c = lax.broadcasted_iota(jnp.int32, (n, n), 1) // group
    return (r == c).astype(BF16)


def _ada_kernel(c_ref, w_ref, b_ref, o_ref):
    s = _silu(c_ref[...]).astype(BF16)
    o_ref[...] = jnp.dot(s, w_ref[0].astype(BF16), preferred_element_type=F32) + b_ref[0]


def ada_proj(c, w, b, layer, tn=512):
    m, k = c.shape
    n = w.shape[2]
    return pl.pallas_call(
        _ada_kernel,
        grid=(n // tn,),
        in_specs=[pl.BlockSpec((m, k), lambda j: (0, 0)),
                  pl.BlockSpec((1, k, tn), lambda j: (layer, 0, j)),
                  pl.BlockSpec((1, 1, tn), lambda j: (layer, 0, j))],
        out_specs=pl.BlockSpec((m, tn), lambda j: (0, j)),
        out_shape=jax.ShapeDtypeStruct((m, n), F32),
        compiler_params=_cparams("parallel"),
        name="ada_proj",
    )(c, w, b.reshape(b.shape[0], 1, n))


def _mod_kernel(x_ref, sc_ref, sh_ref, o_ref):
    o_ref[0] = (x_ref[0] * (1.0 + sc_ref[0]) + sh_ref[0]).astype(o_ref.dtype)


def modulate(x, sc, sh):
    b, t, d = x.shape
    tt = min(t, 512)
    return pl.pallas_call(
        _mod_kernel,
        grid=(b, t // tt),
        in_specs=[pl.BlockSpec((1, tt, d), lambda i, j: (i, j, 0)),
                  pl.BlockSpec((1, 1, d), lambda i, j: (i, 0, 0)),
                  pl.BlockSpec((1, 1, d), lambda i, j: (i, 0, 0))],
        out_specs=pl.BlockSpec((1, tt, d), lambda i, j: (i, j, 0)),
        out_shape=jax.ShapeDtypeStruct((b, t, d), BF16),
        compiler_params=_cparams("parallel", "parallel"),
        name="modulate",
    )(x, sc, sh)


def _mm_kernel(x_ref, w_ref, o_ref):
    o_ref[...] = jnp.dot(x_ref[...], w_ref[...], preferred_element_type=F32).astype(o_ref.dtype)


def matmul(x, w, tn=512, out_dtype=F32):
    m, k = x.shape
    n = w.shape[1]
    tm = min(m, 1024)
    return pl.pallas_call(
        _mm_kernel,
        grid=(m // tm, n // tn),
        in_specs=[pl.BlockSpec((tm, k), lambda i, j: (i, 0)),
                  pl.BlockSpec((k, tn), lambda i, j: (0, j))],
        out_specs=pl.BlockSpec((tm, tn), lambda i, j: (i, j)),
        out_shape=jax.ShapeDtypeStruct((m, n), out_dtype),
        compiler_params=_cparams("parallel", "parallel"),
        name="matmul",
    )(x, w)


def _rw_pre_kernel(p_ref, pv_ref, mix_ref, w0_ref, w2_ref, a0_ref, a2_ref, g2_ref, kk_ref, ka_ref,
                   r_ref, ld_ref, k_ref, v_ref, kn_ref, b_ref, g_ref, *, width):
    p = p_ref[0]
    xm = p + (pv_ref[0] - p) * mix_ref[...]
    r = xm[:, 0:width]
    k = xm[:, width:2 * width]
    v = xm[:, 2 * width:3 * width]
    o = 3 * width
    xw = xm[:, o:o + RW_DECAY_LORA]
    xa = xm[:, o + RW_DECAY_LORA:o + RW_DECAY_LORA + RW_ICLR_LORA]
    o2 = o + RW_DECAY_LORA + RW_ICLR_LORA
    xg = xm[:, o2:o2 + RW_GATE_LORA]
    z = w0_ref[...] + _dot1(jnp.tanh(xw), w2_ref[...])
    softplus = jnp.maximum(-z, 0.0) + jnp.log(1.0 + jnp.exp(-jnp.abs(z)))
    ld_ref[0] = -jnp.exp(-softplus - 0.5)
    a = _sigmoid(a0_ref[...] + _dot1(xa, a2_ref[...]))
    g_ref[0] = _dot1(_sigmoid(xg), g2_ref[...])
    kk = k * kk_ref[...]
    ones = _group_ones(V7X_LANES, RW_HEAD_DIM)
    sq = kk * kk
    for c in range(width // V7X_LANES):
        sl = slice(c * V7X_LANES, (c + 1) * V7X_LANES)
        ss = _dot_exact_rhs(sq[:, sl], ones)
        kn = kk[:, sl] / jnp.maximum(jnp.sqrt(ss), 1e-12)
        kn_ref[0, :, sl] = kn
        b_ref[0, :, sl] = kn * a[:, sl]
    r_ref[0] = r
    v_ref[0] = v
    k_ref[0] = k * (1.0 + (a - 1.0) * ka_ref[...])


def rw_pre(p, prev, mix, w0, w2, a0, a2, g2, k_k, k_a, width):
    b, t, cols = p.shape
    tt = min(t, 256)
    row = lambda a: a.reshape(1, -1)
    full = lambda a: pl.BlockSpec(a.shape, lambda i, j: (0,) * a.ndim)
    tok = pl.BlockSpec((1, tt, cols), lambda i, j: (i, j, 0))
    out = pl.BlockSpec((1, tt, width), lambda i, j: (i, j, 0))
    params = [row(mix), row(w0), w2, row(a0), a2, g2, row(k_k), row(k_a)]
    return pl.pallas_call(
        functools.partial(_rw_pre_kernel, width=width),
        grid=(b, t // tt),
        in_specs=[tok, tok] + [full(a) for a in params],
        out_specs=[out] * 7,
        out_shape=[jax.ShapeDtypeStruct((b, t, width), F32)] * 7,
        compiler_params=_cparams("parallel", "parallel"),
        name="rw_pre",
    )(p, prev, *params)


def _rw_scan_kernel(r_ref, ld_ref, k_ref, v_ref, kn_ref, b_ref, g_ref, s0_ref, rk_ref, lg_ref, lb_ref,
                    o_ref, sT_ref, s_scr):
    c = pl.program_id(2)
    C = r_ref.shape[1]
    W = 2 * RW_HEAD_DIM

    @pl.when(c == 0)
    def _():
        s_scr[...] = s0_ref[0, 0]

    r = r_ref[0]
    ld = ld_ref[0]
    k = k_ref[0]
    v = v_ref[0]
    kn = kn_ref[0]
    bb = b_ref[0]

    rows = lax.broadcasted_iota(jnp.int32, (C, C), 0)
    cols = lax.broadcasted_iota(jnp.int32, (C, C), 1)
    tri = (rows >= cols).astype(BF16)
    h1, h2, h3 = _split3(ld)
    cum = _dg(tri, h1, _NN) + (_dg(tri, h2, _NN) + _dg(tri, h3, _NN))
    cum_end = cum[C - 1:C, :]
    kt = kn * jnp.exp(cum - ld)
    rt = r * jnp.exp(cum)
    ginv = jnp.exp(-cum)
    Kt = k * ginv
    Bt = bb * ginv
    gdec = jnp.exp(cum_end - cum)
    Kd = k * gdec
    Bd = bb * gdec

    head0 = lax.broadcasted_iota(jnp.int32, (C, W), 1) < RW_HEAD_DIM

    def stack(x):
        return jnp.concatenate([jnp.where(head0, x, 0.0), jnp.where(head0, 0.0, x)], axis=0)

    kt_s, rt_s, Kt_s, Bt_s, V_s, Kd_s, Bd_s = [stack(x) for x in (kt, rt, Kt, Bt, v, Kd, Bd)]
    n = 2 * C
    gram = _dot3(jnp.concatenate([kt_s, rt_s], axis=0), jnp.concatenate([Kt_s, Bt_s], axis=0), _NT)
    ri = lax.broadcasted_iota(jnp.int32, (n, n), 0)
    ci = lax.broadcasted_iota(jnp.int32, (n, n), 1)
    strict = ri > ci
    incl = ri >= ci
    M1 = jnp.where(strict, gram[0:n, 0:n], 0.0)
    L = jnp.where(strict, gram[0:n, n:2 * n], 0.0)
    A3 = jnp.where(incl, gram[n:2 * n, 0:n], 0.0)
    A4 = jnp.where(incl, gram[n:2 * n, n:2 * n], 0.0)

    eye = (ri == ci).astype(F32)
    tinv = eye - L
    pw = L
    for _ in range(max(int(math.ceil(math.log2(C))) - 1, 0)):
        pw = _dot3(pw, pw)
        tinv = _dot3(tinv, eye + pw)

    m1v = _dot3(M1, V_s)
    wmat = _dot3(tinv, jnp.concatenate([kt_s, m1v], axis=1))
    a4w = _dot3(A4, wmat)
    rp = rt_s - a4w[:, 0:W]
    o2 = _dot3(A3, V_s) - a4w[:, W:2 * W]
    bdw = _dot3(Bd_s, wmat, _TN)
    kdv = _dot3(Kd_s, V_s, _TN)
    rw = lax.broadcasted_iota(jnp.int32, (W, W), 0)
    cw = lax.broadcasted_iota(jnp.int32, (W, W), 1)
    pm = jnp.where(rw == cw, jnp.exp(cum_end), 0.0) - bdw[:, 0:W]
    q = kdv - bdw[:, W:2 * W]
    rs = _dot3(jnp.concatenate([rp, pm], axis=0), s_scr[...])
    o_s = rs[0:n] + o2
    s_new = rs[n:n + W] + q
    s_scr[...] = s_new
    o = o_s[0:C] + o_s[C:n]

    ones = _group_ones(W, RW_HEAD_DIM)
    inv_n = 1.0 / RW_HEAD_DIM
    mu = _dot_exact_rhs(o, ones) * inv_n
    d = o - mu
    var = _dot_exact_rhs(d * d, ones) * inv_n
    on = d * lax.rsqrt(var + RW_LNX_EPS) * lg_ref[...] + lb_ref[...]
    bonus = _dot_exact_rhs(r * k * rk_ref[...], ones) * v
    o_ref[0] = ((on + bonus) * g_ref[0]).astype(o_ref.dtype)

    @pl.when(c == pl.num_programs(2) - 1)
    def _():
        sT_ref[0, 0] = s_new


def rw_scan(r, ld, k, v, kn, bb, g, s0, rk, lnx_g, lnx_b):
    b, t, width = r.shape
    W = 2 * RW_HEAD_DIM
    npair = width // W
    C = RW_CHUNK
    tok = pl.BlockSpec((1, C, W), lambda i, h, c: (i, c, h))
    par = pl.BlockSpec((1, W), lambda i, h, c: (0, h))
    st = pl.BlockSpec((1, 1, W, W), lambda i, h, c: (i, h, 0, 0))
    return pl.pallas_call(
        _rw_scan_kernel,
        grid=(b, npair, t // C),
        in_specs=[tok] * 7 + [st, par, par, par],
        out_specs=[tok, st],
        out_shape=[jax.ShapeDtypeStruct((b, t, width), BF16), jax.ShapeDtypeStruct((b, npair, W, W), F32)],
        scratch_shapes=[pltpu.VMEM((W, W), F32)],
        compiler_params=_cparams("parallel", "parallel", "arbitrary"),
        name="rw_scan",
    )(r, ld, k, v, kn, bb, g, s0, rk.reshape(1, width), lnx_g.reshape(1, width), lnx_b.reshape(1, width))


def _pack_state(wkv):
    b, h, n, _ = wkv.shape
    st = jnp.swapaxes(wkv, -1, -2).reshape(b, h // 2, 2, n, n)
    z = jnp.zeros_like(st[:, :, 0])
    top = jnp.concatenate([st[:, :, 0], z], axis=-1)
    bot = jnp.concatenate([z, st[:, :, 1]], axis=-1)
    return jnp.concatenate([top, bot], axis=-2)


def _unpack_state(sp):
    b, hp, w, _ = sp.shape
    n = w // 2
    s = jnp.stack([sp[:, :, :n, :n], sp[:, :, n:, n:]], axis=2).reshape(b, hp * 2, n, n)
    return jnp.swapaxes(s, -1, -2)


def rwkv7_group(p, shift0, wkv0, lp, width):
    b, t, cols = p.shape
    rw_cols = 3 * width + RW_DECAY_LORA + RW_ICLR_LORA + RW_GATE_LORA
    pad = cols - rw_cols
    shift0p = jnp.pad(shift0, ((0, 0), (0, pad)))
    prev = jnp.concatenate([shift0p[:, None, :], p[:, :-1]], axis=1)
    mix = jnp.pad(lp['rw_mix'], (0, pad))
    outs = rw_pre(p, prev, mix, lp['rw_w0'], lp['rw_w2'], lp['rw_a0'], lp['rw_a2'], lp['rw_g2'],
                  lp['rw_kk'], lp['rw_ka'], width)
    tp = -(-t // RW_CHUNK) * RW_CHUNK
    if tp != t:
        outs = [jnp.pad(a, ((0, 0), (0, tp - t), (0, 0))) for a in outs]
    o, sT = rw_scan(*outs, _pack_state(wkv0), lp['rw_rk'].reshape(-1), lp['rw_lnx_g'], lp['rw_lnx_b'])
    return o[:, :t], _unpack_state(sT), p[:, -1, :rw_cols]


def _glu_kernel(p_ref, o_ref):
    w = o_ref.shape[-1]
    o_ref[0] = p_ref[0, :, 0:w] * _sigmoid(p_ref[0, :, w:2 * w])


def glu(p):
    b, t, c2 = p.shape
    w = c2 // 2
    tt = min(t, 512)
    return pl.pallas_call(
        _glu_kernel,
        grid=(b, t // tt),
        in_specs=[pl.BlockSpec((1, tt, c2), lambda i, j: (i, j, 0))],
        out_specs=pl.BlockSpec((1, tt, w), lambda i, j: (i, j, 0)),
        out_shape=jax.ShapeDtypeStruct((b, t, w), F32),
        compiler_params=_cparams("parallel", "parallel"),
        name="glu",
    )(p)


def _conv_kernel(main_ref, halo_ref, dw_ref, db_ref, g_ref, b_ref, o_ref, win_ref):
    tt = main_ref.shape[1]
    sub = 16 if tt % 16 == 0 else 8
    win_ref[0:tt, :] = main_ref[0]
    win_ref[tt:tt + CONV_HALO, :] = halo_ref[0]

    for base in range(0, tt, sub):
        acc = dw_ref[0:1, :] * win_ref[base:base + sub, :]
        for j in range(1, CONV_K):
            acc = acc + dw_ref[j:j + 1, :] * win_ref[base + j:base + j + sub, :]
        y = acc + db_ref[...]
        mu = jnp.mean(y, axis=-1, keepdims=True)
        d = y - mu
        var = jnp.mean(d * d, axis=-1, keepdims=True)
        z = d * lax.rsqrt(var + LN_EPS) * g_ref[...] + b_ref[...]
        o_ref[0, base:base + sub, :] = _silu(z).astype(o_ref.dtype)


def conv_ln_swish(u_ext, t, dw, db, ln_g, ln_b):
    b, _, w = u_ext.shape
    tt = min(t, 128)
    row = lambda a: a.reshape(1, -1)
    if t == tt:
        main, halo = u_ext[:, :t], u_ext[:, t:t + CONV_HALO]
        main_spec = pl.BlockSpec((1, tt, w), lambda i, j: (i, 0, 0))
        halo_spec = pl.BlockSpec((1, CONV_HALO, w), lambda i, j: (i, 0, 0))
    else:
        main = halo = u_ext
        main_spec = pl.BlockSpec((1, tt, w), lambda i, j: (i, j, 0))
        halo_spec = pl.BlockSpec((1, CONV_HALO, w), lambda i, j: (i, (j + 1) * (tt // CONV_HALO), 0))
    full = lambda a: pl.BlockSpec(a.shape, lambda i, j: (0,) * a.ndim)
    params = [dw, row(db), row(ln_g), row(ln_b)]
    return pl.pallas_call(
        _conv_kernel,
        grid=(b, t // tt),
        in_specs=[main_spec, halo_spec] + [full(a) for a in params],
        out_specs=pl.BlockSpec((1, tt, w), lambda i, j: (i, j, 0)),
        out_shape=jax.ShapeDtypeStruct((b, t, w), BF16),
        scratch_shapes=[pltpu.VMEM((tt + CONV_HALO, w), F32)],
        compiler_params=_cparams("parallel", "parallel"),
        name="conv_ln_swish",
    )(main, halo, *params)


def conformer_conv_group(p, conv0, lp):
    b, t, _ = p.shape
    u = glu(p)
    w = u.shape[-1]
    u_ext = jnp.concatenate([conv0, u, jnp.zeros((b, CONV_HALO - (CONV_K - 1), w), F32)], axis=1)
    y = conv_ln_swish(u_ext, t, lp['cv_dw'], lp['cv_db'], lp['cv_ln_g'], lp['cv_ln_b'])
    return y, u_ext[:, t:t + CONV_K - 1]


def _select_topk_lanes(gate, n_valid_mask):
    n = float(gate.shape[1])
    lane = lax.broadcasted_iota(jnp.int32, gate.shape, 1).astype(F32)
    sel = jnp.zeros(gate.shape, jnp.bool_)
    g = gate
    for _ in range(MOBA_TOPK):
        m = jnp.max(g, axis=1, keepdims=True)
        first = jnp.min(jnp.where(g == m, lane, n), axis=1, keepdims=True)
        pick = (lane == first) & (m > NEG_INF)
        sel = sel | pick
        g = jnp.where(pick, NEG_INF, g)
    return sel & n_valid_mask


def _moba_prompt_kernel(q_ref, k_ref, v_ref, o_ref, km_ref, m_ref, l_ref, acc_ref, sel_ref):
    qi = pl.program_id(2)
    tq = q_ref.shape[1]
    t = k_ref.shape[1]
    nb = t // MOBA_BLOCK
    scale = ATT_HEAD_DIM ** -0.5

    @pl.when(qi == 0)
    def _():
        km_ref[...] = jnp.zeros(km_ref.shape, F32)
        for j in range(nb):
            blk = k_ref[0, j * MOBA_BLOCK:(j + 1) * MOBA_BLOCK, :]
            km_ref[j:j + 1, :] = jnp.mean(blk, axis=0, keepdims=True)

    q = q_ref[0]
    qb = q.astype(BF16)
    gate = _dg(qb, km_ref[...].astype(BF16), _NT)
    lane = lax.broadcasted_iota(jnp.int32, gate.shape, 1)
    past = lane < qi
    sel_ref[...] = _select_topk_lanes(jnp.where(past, gate, NEG_INF), past).astype(F32)

    start = pl.multiple_of(qi * MOBA_BLOCK, MOBA_BLOCK)
    kb = k_ref[0, pl.ds(start, MOBA_BLOCK), :].astype(BF16)
    vb = v_ref[0, pl.ds(start, MOBA_BLOCK), :].astype(BF16)
    s = _dg(qb, kb, _NT) * scale
    rr = lax.broadcasted_iota(jnp.int32, s.shape, 0)
    cc = lax.broadcasted_iota(jnp.int32, s.shape, 1)
    s = jnp.where(cc <= rr, s, NEG_INF)
    m0 = jnp.max(s, axis=1, keepdims=True)
    p = jnp.exp(s - m0)
    m_ref[...] = m0
    l_ref[...] = jnp.sum(p, axis=1, keepdims=True)
    acc_ref[...] = _dg(p.astype(BF16), vb, _NN)

    for j in range(nb - 1):
        @pl.when(j < qi)
        def _(j=j):
            kj = k_ref[0, j * MOBA_BLOCK:(j + 1) * MOBA_BLOCK, :].astype(BF16)
            vj = v_ref[0, j * MOBA_BLOCK:(j + 1) * MOBA_BLOCK, :].astype(BF16)
            sj = _dg(qb, kj, _NT) * scale
            ok = sel_ref[:, j:j + 1] > 0.5
            sj = jnp.where(ok, sj, NEG_INF)
            m_old = m_ref[...]
            m_new = jnp.maximum(m_old, jnp.max(sj, axis=1, keepdims=True))
            alpha = jnp.exp(m_old - m_new)
            pj = jnp.exp(sj - m_new)
            l_ref[...] = alpha * l_ref[...] + jnp.sum(pj, axis=1, keepdims=True)
            acc_ref[...] = alpha * acc_ref[...] + _dg(pj.astype(BF16), vj, _NN)
            m_ref[...] = m_new

    o_ref[0] = (acc_ref[...] / l_ref[...]).astype(o_ref.dtype)


def moba_prompt(p_at, heads):
    b, t, _ = p_at.shape
    hd = ATT_HEAD_DIM
    tq = MOBA_BLOCK
    nbp = V7X_LANES
    return pl.pallas_call(
        _moba_prompt_kernel,
        grid=(b, heads, t // tq),
        in_specs=[pl.BlockSpec((1, tq, hd), lambda i, h, j: (i, j, h)),
                  pl.BlockSpec((1, t, hd), lambda i, h, j: (i, 0, heads + h)),
                  pl.BlockSpec((1, t, hd), lambda i, h, j: (i, 0, 2 * heads + h))],
        out_specs=pl.BlockSpec((1, tq, hd), lambda i, h, j: (i, j, h)),
        out_shape=jax.ShapeDtypeStruct((b, t, heads * hd), BF16),
        scratch_shapes=[pltpu.VMEM((nbp, hd), F32), pltpu.VMEM((tq, 1), F32), pltpu.VMEM((tq, 1), F32),
                        pltpu.VMEM((tq, hd), F32), pltpu.VMEM((tq, nbp), F32)],
        compiler_params=_cparams("parallel", "parallel", "arbitrary"),
        name="moba_prompt",
    )(p_at, p_at, p_at)


def _kblock_sum_kernel(pt_ref, *refs, heads):
    *k_refs, o_ref = refs
    acc = None
    for k_ref in k_refs:
        x = k_ref[0, 0]
        s = jnp.sum(x.reshape(x.shape[0] // heads, heads, x.shape[1]), axis=0)
        acc = s if acc is None else acc + s
    o_ref[0, 0] = acc


def block_key_sums(ck, layer, page_table, heads, ppb):
    bs, n_pages = page_table.shape
    rows, hd = ck.shape[2:]
    nblk = n_pages // ppb
    page_spec = lambda k: pl.BlockSpec((1, 1, rows, hd), lambda i, j, pt: (layer, pt[i, j * ppb + k], 0, 0))
    return pl.pallas_call(
        functools.partial(_kblock_sum_kernel, heads=heads),
        grid_spec=pltpu.PrefetchScalarGridSpec(
            num_scalar_prefetch=1,
            grid=(bs, nblk),
            in_specs=[page_spec(k) for k in range(ppb)],
            out_specs=pl.BlockSpec((1, 1, heads, hd), lambda i, j, pt: (i, j, 0, 0)),
        ),
        out_shape=jax.ShapeDtypeStruct((bs, nblk, heads, hd), F32),
        compiler_params=_cparams("parallel", "arbitrary"),
        name="block_key_sums",
    )(page_table, *([ck] * ppb))


def _moba_sample_kernel(pt_ref, q_ref, km_ref, kn_ref, vn_ref, k_ref, v_ref, o_ref,
                        m_ref, l_ref, acc_ref, sel_ref, *, heads, t_new, pages_per_block):
    j = pl.program_id(1)
    scale = ATT_HEAD_DIM ** -0.5
    qb = q_ref[0].astype(BF16)
    n_rows = qb.shape[0]

    def head_match(n_lanes):
        row_head = lax.broadcasted_iota(jnp.int32, (n_rows, n_lanes), 0) // t_new
        lane_head = lax.broadcasted_iota(jnp.int32, (n_rows, n_lanes), 1) % heads
        return row_head == lane_head

    @pl.when(j == 0)
    def _():
        km = km_ref[0].astype(BF16)
        g = _dg(qb, km, _NT)
        match = head_match(g.shape[1])
        picked = _select_topk_lanes(jnp.where(match, g, NEG_INF), match).astype(BF16)
        nblk = sel_ref.shape[1]
        fold = (lax.broadcasted_iota(jnp.int32, (g.shape[1], nblk), 0) // heads
                == lax.broadcasted_iota(jnp.int32, (g.shape[1], nblk), 1)).astype(BF16)
        sel_ref[...] = _dg(picked, fold, _NN)
        s = _dg(qb, kn_ref[0].astype(BF16), _NT) * scale
        row_t = lax.broadcasted_iota(jnp.int32, s.shape, 0) % t_new
        lane_t = lax.broadcasted_iota(jnp.int32, s.shape, 1) // heads
        s = jnp.where(head_match(s.shape[1]) & (lane_t <= row_t), s, NEG_INF)
        m0 = jnp.max(s, axis=1, keepdims=True)
        p = jnp.exp(s - m0)
        m_ref[...] = m0
        l_ref[...] = jnp.sum(p, axis=1, keepdims=True)
        acc_ref[...] = _dg(p.astype(BF16), vn_ref[0].astype(BF16), _NN)

    blk = j // pages_per_block
    lane = lax.broadcasted_iota(jnp.int32, sel_ref.shape, 1)
    ok = jnp.sum(jnp.where(lane == blk, sel_ref[...], 0.0), axis=1, keepdims=True) > 0.5
    s = _dg(qb, k_ref[0, 0].astype(BF16), _NT) * scale
    s = jnp.where(ok & head_match(s.shape[1]), s, NEG_INF)
    m_old = m_ref[...]
    m_new = jnp.maximum(m_old, jnp.max(s, axis=1, keepdims=True))
    alpha = jnp.exp(m_old - m_new)
    p = jnp.exp(s - m_new)
    l_ref[...] = alpha * l_ref[...] + jnp.sum(p, axis=1, keepdims=True)
    acc_ref[...] = alpha * acc_ref[...] + _dg(p.astype(BF16), v_ref[0, 0].astype(BF16), _NN)
    m_ref[...] = m_new

    @pl.when(j == pl.num_programs(1) - 1)
    def _():
        o_ref[0] = (acc_ref[...] / l_ref[...]).astype(o_ref.dtype)


def moba_sample(p_at, cache_k, cache_v, layer, page_table, heads):
    b, t, _ = p_at.shape
    hd = ATT_HEAD_DIM
    w = heads * hd
    depth, n_pool, page = cache_k.shape[:3]
    n_pages = page_table.shape[1]
    assert MOBA_BLOCK % page == 0 and (n_pages * page) % MOBA_BLOCK == 0 and t <= MOBA_BLOCK
    assert n_pages * page // MOBA_BLOCK >= MOBA_TOPK
    ppb = MOBA_BLOCK // page
    nblk = n_pages // ppb
    ck = cache_k.reshape(depth, n_pool, page * heads, hd)
    cv = cache_v.reshape(depth, n_pool, page * heads, hd)
    ksum = block_key_sums(ck, layer, page_table, heads, ppb)
    kmean = (ksum * (1.0 / MOBA_BLOCK)).reshape(b, nblk * heads, hd)
    q = p_at[..., :w].reshape(b, t, heads, hd).transpose(0, 2, 1, 3).reshape(b, heads * t, hd)
    kn = p_at[..., w:2 * w].reshape(b, t * heads, hd)
    vn = p_at[..., 2 * w:].reshape(b, t * heads, hd)
    rows = heads * t
    kernel = functools.partial(_moba_sample_kernel, heads=heads, t_new=t, pages_per_block=ppb)
    per_b = lambda r: pl.BlockSpec((1, r, hd), lambda i, j, pt: (i, 0, 0))
    page_spec = pl.BlockSpec((1, 1, page * heads, hd), lambda i, j, pt: (layer, pt[i, j], 0, 0))
    o = pl.pallas_call(
        kernel,
        grid_spec=pltpu.PrefetchScalarGridSpec(
            num_scalar_prefetch=1,
            grid=(b, n_pages),
            in_specs=[per_b(rows), per_b(nblk * heads), per_b(t * heads), per_b(t * heads), page_spec, page_spec],
            out_specs=per_b(rows),
            scratch_shapes=[pltpu.VMEM((rows, 1), F32), pltpu.VMEM((rows, 1), F32),
                            pltpu.VMEM((rows, hd), F32), pltpu.VMEM((rows, nblk), F32)],
        ),
        out_shape=jax.ShapeDtypeStruct((b, rows, hd), BF16),
        compiler_params=_cparams("parallel", "arbitrary"),
        name="moba_sample",
    )(page_table, q, kmean, kn, vn, ck, cv)
    return o.reshape(b, heads, t, hd).transpose(0, 2, 1, 3).reshape(b, t, w)


def _ln_rows(y, g, b):
    mu = jnp.mean(y, axis=-1, keepdims=True)
    d = y - mu
    var = jnp.mean(d * d, axis=-1, keepdims=True)
    return d * lax.rsqrt(var + LN_EPS) * g + b


def _resid_ln_mod_kernel(x_ref, f_ref, gt_ref, g_ref, b_ref, sc_ref, sh_ref, rw_ref, x1_ref, h_ref, lg_ref):
    x1 = _ln_rows(DEEPNORM_ALPHA * x_ref[0] + gt_ref[0] * f_ref[0], g_ref[...], b_ref[...])
    x1_ref[0] = x1
    h = x1 * (1.0 + sc_ref[0]) + sh_ref[0]
    hb = h.astype(h_ref.dtype)
    h_ref[0] = hb
    lg_ref[0] = _dg(hb, rw_ref[...].astype(BF16), _NN)


def resid_ln_mod(x, f, gt, g, b, sc, sh, router_w):
    bsz, t, d = x.shape
    ne = router_w.shape[1]
    tt = min(t, 256)
    tok = pl.BlockSpec((1, tt, d), lambda i, j: (i, j, 0))
    per_b = pl.BlockSpec((1, 1, d), lambda i, j: (i, 0, 0))
    vec = pl.BlockSpec((1, d), lambda i, j: (0, 0))
    return pl.pallas_call(
        _resid_ln_mod_kernel,
        grid=(bsz, t // tt),
        in_specs=[tok, tok, per_b, vec, vec, per_b, per_b, pl.BlockSpec((d, ne), lambda i, j: (0, 0))],
        out_specs=[tok, tok, pl.BlockSpec((1, tt, ne), lambda i, j: (i, j, 0))],
        out_shape=[jax.ShapeDtypeStruct((bsz, t, d), F32), jax.ShapeDtypeStruct((bsz, t, d), BF16),
                   jax.ShapeDtypeStruct((bsz, t, ne), F32)],
        compiler_params=_cparams("parallel", "parallel"),
        name="resid_ln_mod",
    )(x, f, gt, g.reshape(1, d), b.reshape(1, d), sc, sh, router_w)


def _resid_ln_kernel(x_ref, f_ref, gt_ref, g_ref, b_ref, o_ref):
    o_ref[0] = _ln_rows(DEEPNORM_ALPHA * x_ref[0] + gt_ref[0] * f_ref[0], g_ref[...], b_ref[...])


def resid_ln(x, f, gt, g, b):
    bsz, t, d = x.shape
    tt = min(t, 256)
    tok = pl.BlockSpec((1, tt, d), lambda i, j: (i, j, 0))
    per_b = pl.BlockSpec((1, 1, d), lambda i, j: (i, 0, 0))
    vec = pl.BlockSpec((1, d), lambda i, j: (0, 0))
    return pl.pallas_call(
        _resid_ln_kernel,
        grid=(bsz, t // tt),
        in_specs=[tok, tok, per_b, vec, vec],
        out_specs=tok,
        out_shape=jax.ShapeDtypeStruct((bsz, t, d), F32),
        compiler_params=_cparams("parallel", "parallel"),
        name="resid_ln",
    )(x, f, gt, g.reshape(1, d), b.reshape(1, d))


def _moe_up_kernel(be_ref, bf_ref, nu_ref, x_ref, w1_ref, w3_ref, o_ref, w1b_ref, w3b_ref):
    i = pl.program_id(1)

    @pl.when(i < nu_ref[0])
    def _():
        @pl.when(bf_ref[i] == 1)
        def _():
            w1b_ref[...] = w1_ref[0, 0].astype(BF16)
            w3b_ref[...] = w3_ref[0, 0].astype(BF16)

        x = x_ref[...]
        a = jnp.dot(x, w1b_ref[...], preferred_element_type=F32)
        b = jnp.dot(x, w3b_ref[...], preferred_element_type=F32)
        o_ref[...] = (_silu(a) * b).astype(o_ref.dtype)


def _moe_down_kernel(be_ref, bf_ref, nu_ref, h_ref, w2_ref, sg_ref, o_ref, w2b_ref):
    i = pl.program_id(1)

    @pl.when(i < nu_ref[0])
    def _():
        @pl.when(bf_ref[i] == 1)
        def _():
            w2b_ref[...] = w2_ref[0, 0].astype(BF16)

        o_ref[...] = jnp.dot(h_ref[...], w2b_ref[...], preferred_element_type=F32) * sg_ref[...]


def moe_experts(xs, row_gate, blk_e, blk_first, n_used, w1, w3, w2, layer, tf=256, tn=1024):
    cap, d = xs.shape
    ff = w1.shape[-1]
    nblk = cap // MOE_ROWS
    R = MOE_ROWS
    clamp = lambda i, nu: jnp.minimum(i, nu[0] - 1)
    hmid = pl.pallas_call(
        _moe_up_kernel,
        grid_spec=pltpu.PrefetchScalarGridSpec(
            num_scalar_prefetch=3,
            grid=(ff // tf, nblk),
            in_specs=[pl.BlockSpec((R, d), lambda f, i, be, bf, nu: (clamp(i, nu), 0)),
                      pl.BlockSpec((1, 1, d, tf), lambda f, i, be, bf, nu: (layer, be[clamp(i, nu)], 0, f)),
                      pl.BlockSpec((1, 1, d, tf), lambda f, i, be, bf, nu: (layer, be[clamp(i, nu)], 0, f))],
            out_specs=pl.BlockSpec((R, tf), lambda f, i, be, bf, nu: (clamp(i, nu), f)),
            scratch_shapes=[pltpu.VMEM((d, tf), BF16), pltpu.VMEM((d, tf), BF16)],
        ),
        out_shape=jax.ShapeDtypeStruct((cap, ff), BF16),
        compiler_params=_cparams("arbitrary", "arbitrary"),
        name="moe_up",
    )(blk_e, blk_first, n_used, xs, w1, w3)
    return pl.pallas_call(
        _moe_down_kernel,
        grid_spec=pltpu.PrefetchScalarGridSpec(
            num_scalar_prefetch=3,
            grid=(d // tn, nblk),
            in_specs=[pl.BlockSpec((R, ff), lambda n, i, be, bf, nu: (clamp(i, nu), 0)),
                      pl.BlockSpec((1, 1, ff, tn), lambda n, i, be, bf, nu: (layer, be[clamp(i, nu)], 0, n)),
                      pl.BlockSpec((R, 1), lambda n, i, be, bf, nu: (clamp(i, nu), 0))],
            out_specs=pl.BlockSpec((R, tn), lambda n, i, be, bf, nu: (clamp(i, nu), n)),
            scratch_shapes=[pltpu.VMEM((ff, tn), BF16)],
        ),
        out_shape=jax.ShapeDtypeStruct((cap, d), F32),
        compiler_params=_cparams("arbitrary", "arbitrary"),
        name="moe_down",
    )(blk_e, blk_first, n_used, hmid, w2, row_gate)


def moe_ffn(h, logits, router_b, w1, w3, w2, layer):
    n_tok, d = h.shape
    probs = jax.nn.softmax(logits, axis=-1)
    sel = (probs + router_b).reshape(n_tok, N_EXPERT_GROUPS, EXPERTS_PER_GROUP)
    assert TOP_K == 2
    first = jnp.argmax(sel, axis=-1)
    rest = jnp.where(jnp.arange(EXPERTS_PER_GROUP) == first[..., None], NEG_INF, sel)
    grp_score = jnp.max(sel, axis=-1) + jnp.max(rest, axis=-1)
    best = jnp.argmax(grp_score, axis=-1)
    in_grp = jnp.take_along_axis(sel, best[:, None, None], axis=1)[:, 0]
    _, local = lax.top_k(in_grp, TOP_K)
    eidx = best[:, None] * EXPERTS_PER_GROUP + local
    gates = jnp.take_along_axis(probs, eidx, axis=1)
    gates = gates / jnp.sum(gates, axis=-1, keepdims=True)

    n_asg = n_tok * TOP_K
    flat_e = eidx.reshape(n_asg).astype(jnp.int32)
    flat_g = gates.reshape(n_asg)
    order = jnp.argsort(flat_e)
    se = flat_e[order]
    counts = jnp.bincount(flat_e, length=N_EXPERTS).astype(jnp.int32)
    starts = jnp.cumsum(counts) - counts
    pcounts = (counts + MOE_ROWS - 1) // MOE_ROWS * MOE_ROWS
    pends = jnp.cumsum(pcounts)
    pstarts = pends - pcounts
    dest = (pstarts[se] + jnp.arange(n_asg, dtype=jnp.int32) - starts[se]).astype(jnp.int32)
    n_blk = -(-(n_asg + N_EXPERTS * (MOE_ROWS - 1)) // MOE_ROWS)
    cap = n_blk * MOE_ROWS
    buf_tok = jnp.zeros((cap,), jnp.int32).at[dest].set((order // TOP_K).astype(jnp.int32))
    buf_gate = jnp.zeros((cap,), F32).at[dest].set(flat_g[order])
    pos = jnp.zeros((n_asg,), jnp.int32).at[order].set(dest)
    blk_start = jnp.arange(n_blk, dtype=jnp.int32) * MOE_ROWS
    blk_e = jnp.minimum(jnp.sum(pends[None, :] <= blk_start[:, None], axis=1), N_EXPERTS - 1).astype(jnp.int32)
    blk_first = jnp.concatenate([jnp.ones((1,), jnp.int32), (blk_e[1:] != blk_e[:-1]).astype(jnp.int32)])
    n_used = (pends[-1:] // MOE_ROWS).astype(jnp.int32)

    xs = h[buf_tok]
    ys = moe_experts(xs, buf_gate[:, None], blk_e, blk_first, n_used, w1, w3, w2, layer)
    pos = pos.reshape(n_tok, TOP_K)
    return ys[pos[:, 0]] + ys[pos[:, 1]]


def _mixing_sublayer(x, ada, shift0, wkv0, conv0, kv_past, lp, wts, router_w):
    b, t, d = x.shape
    sh1, sc1, gt1, sh2, sc2 = [ada[:, i:i + 1] for i in range(5)]
    h = modulate(x, sc1, sh1).reshape(b * t, d)
    width = d // 4
    heads = (d - 2 * width) // ATT_HEAD_DIM
    p_rw = matmul(h, wts['w_rw']).reshape(b, t, -1)
    p_cv = matmul(h, wts['w_cv']).reshape(b, t, -1)
    p_at = matmul(h, wts['w_at']).reshape(b, t, -1)
    o_rw, wkv_new, shift_new = rwkv7_group(p_rw, shift0, wkv0, lp, width)
    o_cv, conv_new = conformer_conv_group(p_cv, conv0, lp)
    if kv_past is None:
        o_at = moba_prompt(p_at, heads)
    else:
        o_at = moba_sample(p_at, *kv_past, heads)
    aw = heads * ATT_HEAD_DIM
    k_new = p_at[..., aw:2 * aw].reshape(b, t, heads, ATT_HEAD_DIM)
    v_new = p_at[..., 2 * aw:].reshape(b, t, heads, ATT_HEAD_DIM)
    mixed = matmul(jnp.concatenate([o_rw, o_cv, o_at], axis=-1).reshape(b * t, d), wts['w_out']).reshape(b, t, d)
    x1, h2, logits = resid_ln_mod(x, mixed, gt1, lp['ln1_g'], lp['ln1_b'], sc2, sh2, router_w)
    return x1, h2, logits, (k_new, v_new, wkv_new, shift_new, conv_new)


def kernel(x_prompt, x_sample, cache_k, cache_v, page_table, state_wkv, state_shift, state_conv,
           c_prompt, c_sample, w_ada, b_ada, w_in, w_out, rw_mix, rw_w0, rw_w2, rw_a0, rw_a2, rw_g2,
           rw_kk, rw_ka, rw_rk, rw_lnx_g, rw_lnx_b, cv_dw, cv_db, cv_ln_g, cv_ln_b,
           ln1_g, ln1_b, ln2_g, ln2_b, router_w, router_b, ex_w1, ex_w3, ex_w2):
    bp, tp, d = x_prompt.shape
    bs, ts, _ = x_sample.shape
    depth = w_in.shape[0]
    width = d // 4
    rw_cols = 3 * width + RW_DECAY_LORA + RW_ICLR_LORA + RW_GATE_LORA
    rw_pad = -(-rw_cols // 512) * 512
    at0 = rw_cols + 2 * width
    heads_rw = width // RW_HEAD_DIM

    c_all = jnp.concatenate([c_prompt, c_sample], axis=0)
    c_rows = -(-c_all.shape[0] // 8) * 8
    c_all = jnp.pad(c_all, ((0, c_rows - c_all.shape[0]), (0, 0)))

    yp, ys = x_prompt, x_sample
    st_p, st_s = [], []
    for l in range(depth):
        lp = {'rw_mix': rw_mix[l], 'rw_w0': rw_w0[l], 'rw_w2': rw_w2[l], 'rw_a0': rw_a0[l], 'rw_a2': rw_a2[l],
              'rw_g2': rw_g2[l], 'rw_kk': rw_kk[l], 'rw_ka': rw_ka[l], 'rw_rk': rw_rk[l],
              'rw_lnx_g': rw_lnx_g[l], 'rw_lnx_b': rw_lnx_b[l],
              'cv_dw': cv_dw[l], 'cv_db': cv_db[l], 'cv_ln_g': cv_ln_g[l], 'cv_ln_b': cv_ln_b[l],
              'ln1_g': ln1_g[l], 'ln1_b': ln1_b[l]}
        wl = w_in[l]
        wts = {'w_rw': jnp.pad(wl[:, :rw_cols].astype(BF16), ((0, 0), (0, rw_pad - rw_cols))),
               'w_cv': wl[:, rw_cols:at0].astype(BF16),
               'w_at': wl[:, at0:].astype(BF16),
               'w_out': w_out[l].astype(BF16)}
        ada = ada_proj(c_all, w_ada, b_ada, l).reshape(c_rows, 6, d)
        ada_p, ada_s = ada[:bp], ada[bp:bp + bs]

        zeros_p = (jnp.zeros((bp, rw_cols), F32), jnp.zeros((bp, heads_rw, RW_HEAD_DIM, RW_HEAD_DIM), F32),
                   jnp.zeros((bp, CONV_K - 1, width), F32))
        xp1, hp2, lgp, sp = _mixing_sublayer(yp, ada_p, *zeros_p, None, lp, wts, router_w)
        xs1, hs2, lgs, ss = _mixing_sublayer(ys, ada_s, state_shift[l], state_wkv[l], state_conv[l],
                                             (cache_k, cache_v, l, page_table), lp, wts, router_w)
        h_all = jnp.concatenate([hp2.reshape(bp * tp, d), hs2.reshape(bs * ts, d)], axis=0)
        lg_all = jnp.concatenate([lgp.reshape(bp * tp, -1), lgs.reshape(bs * ts, -1)], axis=0)
        f_all = moe_ffn(h_all, lg_all, router_b, ex_w1, ex_w3, ex_w2, l)
        yp = resid_ln(xp1, f_all[:bp * tp].reshape(bp, tp, d), ada_p[:, 5:6], ln2_g[l], ln2_b[l])
        ys = resid_ln(xs1, f_all[bp * tp:].reshape(bs, ts, d), ada_s[:, 5:6], ln2_g[l], ln2_b[l])
        st_p.append(sp)
        st_s.append(ss)
    k_p, v_p, wkv_p, shift_p, conv_p = [jnp.stack([s[i] for s in st_p]) for i in range(5)]
    k_s, v_s, wkv_s, shift_s, conv_s = [jnp.stack([s[i] for s in st_s]) for i in range(5)]
    return (yp, ys, k_p, v_p, wkv_p, shift_p, conv_p, k_s, v_s, wkv_s, shift_s, conv_s)
```

```python
import functools
import math

import jax
import jax.numpy as jnp
from jax import lax
from jax.experimental import pallas as pl
from jax.experimental.pallas import tpu as pltpu

F32 = jnp.float32
BF16 = jnp.bfloat16

V7X_LANES = 128
V7X_VMEM_BYTES = 64 * 1024 * 1024
VMEM_LIMIT = V7X_VMEM_BYTES - 8 * 1024 * 1024

DEPTH = 2
RW_HEAD_DIM = 64
RW_DECAY_LORA = 64
RW_ICLR_LORA = 64
RW_GATE_LORA = 160
RW_LNX_EPS = 64e-5
CONV_K = 31
ATT_HEAD_DIM = 128
MOBA_BLOCK = 256
MOBA_TOPK = 3
N_EXPERTS = 32
N_EXPERT_GROUPS = 4
EXPERTS_PER_GROUP = N_EXPERTS // N_EXPERT_GROUPS
TOP_K = 2
DEEPNORM_ALPHA = (2 * DEPTH) ** 0.25
LN_EPS = 1e-5

RW_CHUNK = 64
CONV_HALO = 32
MOE_ROWS = 256
NEG_INF = float("-inf")


def _cparams(*sem):
    return pltpu.CompilerParams(dimension_semantics=sem, vmem_limit_bytes=VMEM_LIMIT)


def _sigmoid(x):
    return 1.0 / (1.0 + jnp.exp(-x))


def _silu(x):
    return x * _sigmoid(x)


def _split2(x):
    hi = x.astype(BF16)
    lo = (x - hi.astype(F32)).astype(BF16)
    return hi, lo


def _split3(x):
    hi = x.astype(BF16)
    r1 = x - hi.astype(F32)
    mid = r1.astype(BF16)
    lo = (r1 - mid.astype(F32)).astype(BF16)
    return hi, mid, lo


_NN = (((1,), (0,)), ((), ()))
_NT = (((1,), (1,)), ((), ()))
_TN = (((0,), (0,)), ((), ()))


def _dg(a, b, dims):
    return lax.dot_general(a, b, dims, preferred_element_type=F32)


def _dot1(a, b, dims=_NN):
    return _dg(a.astype(BF16), b.astype(BF16), dims)


def _dot3(a, b, dims=_NN):
    ah, al = _split2(a)
    bh, bl = _split2(b)
    return _dg(ah, bh, dims) + (_dg(ah, bl, dims) + _dg(al, bh, dims))


def _dot_exact_rhs(a, b_bf16, dims=_NN):
    hi, mid, lo = _split3(a)
    return _dg(hi, b_bf16, dims) + (_dg(mid, b_bf16, dims) + _dg(lo, b_bf16, dims))


def _group_ones(n, group):
    r = lax.broadcasted_iota(jnp.int32, (n, n), 0) // group
    c = lax.broadcasted_iota(jnp.int32, (n, n), 1) // group
    return (r == c).astype(BF16)


def _ada_kernel(c_ref, w_ref, b_ref, o_ref):
    s = _silu(c_ref[...]).astype(BF16)
    o_ref[...] = jnp.dot(s, w_ref[0].astype(BF16), preferred_element_type=F32) + b_ref[0]


def ada_proj(c, w, b, layer, tn=512):
    m, k = c.shape
    n = w.shape[2]
    return pl.pallas_call(
        _ada_kernel,
        grid=(n // tn,),
        in_specs=[pl.BlockSpec((m, k), lambda j: (0, 0)),
                  pl.BlockSpec((1, k, tn), lambda j: (layer, 0, j)),
                  pl.BlockSpec((1, 1, tn), lambda j: (layer, 0, j))],
        out_specs=pl.BlockSpec((m, tn), lambda j: (0, j)),
        out_shape=jax.ShapeDtypeStruct((m, n), F32),
        compiler_params=_cparams("parallel"),
        name="ada_proj",
    )(c, w, b.reshape(b.shape[0], 1, n))


def _mod_kernel(x_ref, sc_ref, sh_ref, o_ref):
    o_ref[0] = (x_ref[0] * (1.0 + sc_ref[0]) + sh_ref[0]).astype(o_ref.dtype)


def modulate(x, sc, sh):
    b, t, d = x.shape
    tt = min(t, 512)
    return pl.pallas_call(
        _mod_kernel,
        grid=(b, t // tt),
        in_specs=[pl.BlockSpec((1, tt, d), lambda i, j: (i, j, 0)),
                  pl.BlockSpec((1, 1, d), lambda i, j: (i, 0, 0)),
                  pl.BlockSpec((1, 1, d), lambda i, j: (i, 0, 0))],
        out_specs=pl.BlockSpec((1, tt, d), lambda i, j: (i, j, 0)),
        out_shape=jax.ShapeDtypeStruct((b, t, d), BF16),
        compiler_params=_cparams("parallel", "parallel"),
        name="modulate",
    )(x, sc, sh)


def _mm_kernel(x_ref, w_ref, o_ref):
    o_ref[...] = jnp.dot(x_ref[...], w_ref[...], preferred_element_type=F32).astype(o_ref.dtype)


def matmul(x, w, tn=512, out_dtype=F32):
    m, k = x.shape
    n = w.shape[1]
    tm = min(m, 1024)
    return pl.pallas_call(
        _mm_kernel,
        grid=(m // tm, n // tn),
        in_specs=[pl.BlockSpec((tm, k), lambda i, j: (i, 0)),
                  pl.BlockSpec((k, tn), lambda i, j: (0, j))],
        out_specs=pl.BlockSpec((tm, tn), lambda i, j: (i, j)),
        out_shape=jax.ShapeDtypeStruct((m, n), out_dtype),
        compiler_params=_cparams("parallel", "parallel"),
        name="matmul",
    )(x, w)


def _rw_pre_kernel(p_ref, pv_ref, mix_ref, w0_ref, w2_ref, a0_ref, a2_ref, g2_ref, kk_ref, ka_ref,
                   r_ref, ld_ref, k_ref, v_ref, kn_ref, b_ref, g_ref, *, width):
    p = p_ref[0]
    xm = p + (pv_ref[0] - p) * mix_ref[...]
    r = xm[:, 0:width]
    k = xm[:, width:2 * width]
    v = xm[:, 2 * width:3 * width]
    o = 3 * width
    xw = xm[:, o:o + RW_DECAY_LORA]
    xa = xm[:, o + RW_DECAY_LORA:o + RW_DECAY_LORA + RW_ICLR_LORA]
    o2 = o + RW_DECAY_LORA + RW_ICLR_LORA
    xg = xm[:, o2:o2 + RW_GATE_LORA]
    z = w0_ref[...] + _dot1(jnp.tanh(xw), w2_ref[...])
    softplus = jnp.maximum(-z, 0.0) + jnp.log(1.0 + jnp.exp(-jnp.abs(z)))
    ld_ref[0] = -jnp.exp(-softplus - 0.5)
    a = _sigmoid(a0_ref[...] + _dot1(xa, a2_ref[...]))
    g_ref[0] = _dot1(_sigmoid(xg), g2_ref[...])
    kk = k * kk_ref[...]
    ones = _group_ones(V7X_LANES, RW_HEAD_DIM)
    sq = kk * kk
    for c in range(width // V7X_LANES):
        sl = slice(c * V7X_LANES, (c + 1) * V7X_LANES)
        ss = _dot_exact_rhs(sq[:, sl], ones)
        kn = kk[:, sl] / jnp.maximum(jnp.sqrt(ss), 1e-12)
        kn_ref[0, :, sl] = kn
        b_ref[0, :, sl] = kn * a[:, sl]
    r_ref[0] = r
    v_ref[0] = v
    k_ref[0] = k * (1.0 + (a - 1.0) * ka_ref[...])


def rw_pre(p, prev, mix, w0, w2, a0, a2, g2, k_k, k_a, width):
    b, t, cols = p.shape
    tt = min(t, 256)
    row = lambda a: a.reshape(1, -1)
    full = lambda a: pl.BlockSpec(a.shape, lambda i, j: (0,) * a.ndim)
    tok = pl.BlockSpec((1, tt, cols), lambda i, j: (i, j, 0))
    out = pl.BlockSpec((1, tt, width), lambda i, j: (i, j, 0))
    params = [row(mix), row(w0), w2, row(a0), a2, g2, row(k_k), row(k_a)]
    return pl.pallas_call(
        functools.partial(_rw_pre_kernel, width=width),
        grid=(b, t // tt),
        in_specs=[tok, tok] + [full(a) for a in params],
        out_specs=[out] * 7,
        out_shape=[jax.ShapeDtypeStruct((b, t, width), F32)] * 7,
        compiler_params=_cparams("parallel", "parallel"),
        name="rw_pre",
    )(p, prev, *params)


def _rw_chunk(r, ld, k, v, kn, bb, g, rk, lnx_g, lnx_b, s_prev):
    C = r.shape[0]
    W = 2 * RW_HEAD_DIM

    rows = lax.broadcasted_iota(jnp.int32, (C, C), 0)
    cols = lax.broadcasted_iota(jnp.int32, (C, C), 1)
    tri = (rows >= cols).astype(BF16)
    h1, h2, h3 = _split3(ld)
    cum = _dg(tri, h1, _NN) + (_dg(tri, h2, _NN) + _dg(tri, h3, _NN))
    yield None
    cum_end = cum[C - 1:C, :]
    kt = kn * jnp.exp(cum - ld)
    rt = r * jnp.exp(cum)
    ginv = jnp.exp(-cum)
    Kt = k * ginv
    Bt = bb * ginv
    gdec = jnp.exp(cum_end - cum)
    Kd = k * gdec
    Bd = bb * gdec

    head0 = lax.broadcasted_iota(jnp.int32, (C, W), 1) < RW_HEAD_DIM

    def stack(x):
        return jnp.concatenate([jnp.where(head0, x, 0.0), jnp.where(head0, 0.0, x)], axis=0)

    kt_s, rt_s, Kt_s, Bt_s, V_s, Kd_s, Bd_s = [stack(x) for x in (kt, rt, Kt, Bt, v, Kd, Bd)]
    n = 2 * C
    gram = _dot3(jnp.concatenate([kt_s, rt_s], axis=0), jnp.concatenate([Kt_s, Bt_s], axis=0), _NT)
    yield None
    ri =lax.broadcasted_iota(jnp.int32, (n, n), 0)
    ci = lax.broadcasted_iota(jnp.int32, (n, n), 1)
    strict = ri > ci
    incl = ri >= ci
    M1 = jnp.where(strict, gram[0:n, 0:n], 0.0)
    L = jnp.where(strict, gram[0:n, n:2 * n], 0.0)
    A3 = jnp.where(incl, gram[n:2 * n, 0:n], 0.0)
    A4 = jnp.where(incl, gram[n:2 * n, n:2 * n], 0.0)

    eye = (ri == ci).astype(F32)
    tinv = eye - L
    pw = L
    for _ in range(max(int(math.ceil(math.log2(C))) - 1, 0)):
        pw = _dot3(pw, pw)
        yield None
        tinv = _dot3(tinv, eye + pw)

    m1v = _dot3(M1, V_s)
    a3v = _dot3(A3, V_s)
    kdv = _dot3(Kd_s, V_s, _TN)
    yield None
    wmat = _dot3(tinv, jnp.concatenate([kt_s, m1v], axis=1))
    yield None
    a4w = _dot3(A4, wmat)
    bdw = _dot3(Bd_s, wmat, _TN)
    yield None
    rp = rt_s - a4w[:, 0:W]
    o2 = a3v - a4w[:, W:2 * W]
    rw = lax.broadcasted_iota(jnp.int32, (W, W), 0)
    cw = lax.broadcasted_iota(jnp.int32, (W, W), 1)
    pm = jnp.where(rw == cw, jnp.exp(cum_end), 0.0) - bdw[:, 0:W]
    q = kdv - bdw[:, W:2 * W]
    rs = _dot3(jnp.concatenate([rp, pm], axis=0), s_prev)
    yield None
    o_s = rs[0:n] + o2
    s_new = rs[n:n + W] + q
    o = o_s[0:C] + o_s[C:n]

    ones = _group_ones(W, RW_HEAD_DIM)
    inv_n = 1.0 / RW_HEAD_DIM
    mu = _dot_exact_rhs(o, ones) * inv_n
    bonus = _dot_exact_rhs(r * k * rk, ones) * v
    yield None
    d = o - mu
    var = _dot_exact_rhs(d * d, ones) * inv_n
    yield None
    on = d * lax.rsqrt(var + RW_LNX_EPS) * lnx_g + lnx_b
    yield ((on + bonus) * g).astype(BF16), s_new


def _rw_scan_kernel(r_ref, ld_ref, k_ref, v_ref, kn_ref, b_ref, g_ref, s0_ref, rk_ref, lg_ref, lb_ref,
                    o_ref, sT_ref, s_scr):
    c = pl.program_id(2)
    W = 2 * RW_HEAD_DIM

    @pl.when(c == 0)
    def _():
        s_scr[...] = s0_ref[0]

    chains = []
    for i in range(r_ref.shape[2] // W):
        sl = slice(i * W, (i + 1) * W)
        tok = [ref[0, :, sl] for ref in (r_ref, ld_ref, k_ref, v_ref, kn_ref, b_ref, g_ref)]
        chains.append(_rw_chunk(*tok, rk_ref[:, sl], lg_ref[:, sl], lb_ref[:, sl], s_scr[i]))
    results = [None] * len(chains)
    while any(res is None for res in results):
        for i, chain in enumerate(chains):
            if results[i] is None:
                results[i] = next(chain)
    for i, (o, s_new) in enumerate(results):
        o_ref[0, :, i * W:(i + 1) * W] = o
        s_scr[i] = s_new

    @pl.when(c == pl.num_programs(2) - 1)
    def _():
        sT_ref[0] = s_scr[...]


def rw_scan(r, ld, k, v, kn, bb, g, s0, rk, lnx_g, lnx_b, pairs_per_step=4):
    b, t, width = r.shape
    W = 2 * RW_HEAD_DIM
    npair = width // W
    C = RW_CHUNK
    P = pairs_per_step
    tok = pl.BlockSpec((1, C, P * W), lambda i, h, c: (i, c, h))
    par = pl.BlockSpec((1, P * W), lambda i, h, c: (0, h))
    st = pl.BlockSpec((1, P, W, W), lambda i, h, c: (i, h, 0, 0))
    return pl.pallas_call(
        _rw_scan_kernel,
        grid=(b, npair // P, t // C),
        in_specs=[tok] * 7 + [st, par, par, par],
        out_specs=[tok, st],
        out_shape=[jax.ShapeDtypeStruct((b, t, width), BF16), jax.ShapeDtypeStruct((b, npair, W, W), F32)],
        scratch_shapes=[pltpu.VMEM((P, W, W), F32)],
        compiler_params=_cparams("parallel", "parallel", "arbitrary"),
        name="rw_scan",
    )(r, ld, k, v, kn, bb, g, s0, rk.reshape(1, width), lnx_g.reshape(1, width), lnx_b.reshape(1, width))


def _pack_state(wkv):
    b, h, n, _ = wkv.shape
    st = jnp.swapaxes(wkv, -1, -2).reshape(b, h // 2, 2, n, n)
    z = jnp.zeros_like(st[:, :, 0])
    top = jnp.concatenate([st[:, :, 0], z], axis=-1)
    bot = jnp.concatenate([z, st[:, :, 1]], axis=-1)
    return jnp.concatenate([top, bot], axis=-2)


def _unpack_state(sp):
    b, hp, w, _ = sp.shape
    n = w // 2
    s = jnp.stack([sp[:, :, :n, :n], sp[:, :, n:, n:]], axis=2).reshape(b, hp * 2, n, n)
    return jnp.swapaxes(s, -1, -2)


def rwkv7_group(p, shift0, wkv0, lp, width):
    b, t, cols = p.shape
    rw_cols = 3 * width + RW_DECAY_LORA + RW_ICLR_LORA + RW_GATE_LORA
    pad = cols - rw_cols
    shift0p = jnp.pad(shift0, ((0, 0), (0, pad)))
    prev = jnp.concatenate([shift0p[:, None, :], p[:, :-1]], axis=1)
    mix = jnp.pad(lp['rw_mix'], (0, pad))
    outs = rw_pre(p, prev, mix, lp['rw_w0'], lp['rw_w2'], lp['rw_a0'], lp['rw_a2'], lp['rw_g2'],
                  lp['rw_kk'], lp['rw_ka'], width)
    tp = -(-t // RW_CHUNK) * RW_CHUNK
    if tp != t:
        outs = [jnp.pad(a, ((0, 0), (0, tp - t), (0, 0))) for a in outs]
    o, sT = rw_scan(*outs, _pack_state(wkv0), lp['rw_rk'].reshape(-1), lp['rw_lnx_g'], lp['rw_lnx_b'])
    return o[:, :t], _unpack_state(sT), p[:, -1, :rw_cols]


def _glu_kernel(p_ref, o_ref):
    w = o_ref.shape[-1]
    o_ref[0] = p_ref[0, :, 0:w] * _sigmoid(p_ref[0, :, w:2 * w])


def glu(p):
    b, t, c2 = p.shape
    w = c2 // 2
    tt = min(t, 512)
    return pl.pallas_call(
        _glu_kernel,
        grid=(b, t // tt),
        in_specs=[pl.BlockSpec((1, tt, c2), lambda i, j: (i, j, 0))],
        out_specs=pl.BlockSpec((1, tt, w), lambda i, j: (i, j, 0)),
        out_shape=jax.ShapeDtypeStruct((b, t, w), F32),
        compiler_params=_cparams("parallel", "parallel"),
        name="glu",
    )(p)


def _conv_kernel(main_ref, halo_ref, dw_ref, db_ref, g_ref, b_ref, o_ref, win_ref):
    tt = main_ref.shape[1]
    sub = 16 if tt % 16 == 0 else 8
    win_ref[0:tt, :] = main_ref[0]
    win_ref[tt:tt + CONV_HALO, :] = halo_ref[0]

    for base in range(0, tt, sub):
        acc = dw_ref[0:1, :] * win_ref[base:base + sub, :]
        for j in range(1, CONV_K):
            acc = acc + dw_ref[j:j + 1, :] * win_ref[base + j:base + j + sub, :]
        y = acc + db_ref[...]
        mu = jnp.mean(y, axis=-1, keepdims=True)
        d = y - mu
        var = jnp.mean(d * d, axis=-1, keepdims=True)
        z = d * lax.rsqrt(var + LN_EPS) * g_ref[...] + b_ref[...]
        o_ref[0, base:base + sub, :] = _silu(z).astype(o_ref.dtype)


def conv_ln_swish(u_ext, t, dw, db, ln_g, ln_b):
    b, _, w = u_ext.shape
    tt = min(t, 128)
    row = lambda a: a.reshape(1, -1)
    if t == tt:
        main, halo = u_ext[:, :t], u_ext[:, t:t + CONV_HALO]
        main_spec = pl.BlockSpec((1, tt, w), lambda i, j: (i, 0, 0))
        halo_spec = pl.BlockSpec((1, CONV_HALO, w), lambda i, j: (i, 0, 0))
    else:
        main = halo = u_ext
        main_spec = pl.BlockSpec((1, tt, w), lambda i, j: (i, j, 0))
        halo_spec = pl.BlockSpec((1, CONV_HALO, w), lambda i, j: (i, (j + 1) * (tt // CONV_HALO), 0))
    full = lambda a: pl.BlockSpec(a.shape, lambda i, j: (0,) * a.ndim)
    params = [dw, row(db), row(ln_g), row(ln_b)]
    return pl.pallas_call(
        _conv_kernel,
        grid=(b, t // tt),
        in_specs=[main_spec, halo_spec] + [full(a) for a in params],
        out_specs=pl.BlockSpec((1, tt, w), lambda i, j: (i, j, 0)),
        out_shape=jax.ShapeDtypeStruct((b, t, w), BF16),
        scratch_shapes=[pltpu.VMEM((tt + CONV_HALO, w), F32)],
        compiler_params=_cparams("parallel", "parallel"),
        name="conv_ln_swish",
    )(main, halo, *params)


def conformer_conv_group(p, conv0, lp):
    b, t, _ = p.shape
    u = glu(p)
    w = u.shape[-1]
    u_ext = jnp.concatenate([conv0, u, jnp.zeros((b, CONV_HALO - (CONV_K - 1), w), F32)], axis=1)
    y = conv_ln_swish(u_ext, t, lp['cv_dw'], lp['cv_db'], lp['cv_ln_g'], lp['cv_ln_b'])
    return y, u_ext[:, t:t + CONV_K - 1]


def _select_topk_lanes_staged(gate, n_valid_mask):
    n = float(gate.shape[1])
    lane = lax.broadcasted_iota(jnp.int32, gate.shape, 1).astype(F32)
    sel = jnp.zeros(gate.shape, jnp.bool_)
    g = gate
    for _ in range(MOBA_TOPK):
        m = jnp.max(g, axis=1, keepdims=True)
        yield None
        first = jnp.min(jnp.where(g == m, lane, n), axis=1, keepdims=True)
        yield None
        pick = (lane == first) & (m > NEG_INF)
        sel = sel | pick
        g = jnp.where(pick, NEG_INF, g)
    return sel & n_valid_mask


def _select_topk_lanes(gate, n_valid_mask):
    staged = _select_topk_lanes_staged(gate, n_valid_mask)
    while True:
        try:
            next(staged)
        except StopIteration as done:
            return done.value


def _round_robin(chains):
    live = list(chains)
    while live:
        for chain in list(live):
            try:
                next(chain)
            except StopIteration:
                live.remove(chain)


def _moba_prompt_kernel(q_ref, k_ref, v_ref, o_ref, km_ref, kb_ref, vb_ref):
    qi = pl.program_id(2)
    tq = q_ref.shape[1]
    t = k_ref.shape[1]
    hd = ATT_HEAD_DIM
    n_heads = q_ref.shape[2] // hd
    nb = t // MOBA_BLOCK
    scale = ATT_HEAD_DIM ** -0.5

    @pl.when(qi == 0)
    def _():
        km_ref[...] = jnp.zeros(km_ref.shape, F32)
        for h in range(n_heads):
            for j in range(nb):
                blk = k_ref[0, j * MOBA_BLOCK:(j + 1) * MOBA_BLOCK, h * hd:(h + 1) * hd]
                km_ref[h, j:j + 1, :] = jnp.mean(blk, axis=0, keepdims=True)
        kb_ref[...] = k_ref[0].astype(BF16)
        vb_ref[...] = v_ref[0].astype(BF16)

    def head_chain(h, tk):
        cols = slice(h * hd, (h + 1) * hd)
        qb = q_ref[0, :, cols].astype(BF16)
        gate = _dg(qb, km_ref[h].astype(BF16), _NT)
        yield None
        lane = lax.broadcasted_iota(jnp.int32, gate.shape, 1)
        past = lane < qi
        sel = yield from _select_topk_lanes_staged(jnp.where(past, gate, NEG_INF), past)
        sel = sel.astype(BF16)
        nbp = sel.shape[1]
        expand = (lax.broadcasted_iota(jnp.int32, (nbp, tk), 0)
                  == lax.broadcasted_iota(jnp.int32, (nbp, tk), 1) // MOBA_BLOCK).astype(BF16)
        picked = _dg(sel, expand, _NN) > 0.5
        s = _dg(qb, kb_ref[0:tk, cols], _NT) * scale
        yield None
        row = lax.broadcasted_iota(jnp.int32, (tq, tk), 0)
        key = lax.broadcasted_iota(jnp.int32, (tq, tk), 1)
        local = key - qi * MOBA_BLOCK
        allowed = ((local >= 0) & (local <= row)) | picked
        s = jnp.where(allowed, s, NEG_INF)
        m = jnp.max(s, axis=1, keepdims=True)
        yield None
        p = jnp.exp(s - m)
        l = jnp.sum(p, axis=1, keepdims=True)
        pv = _dg(p.astype(BF16), vb_ref[0:tk, cols], _NN)
        yield None
        o_ref[0, :, cols] = (pv / l).astype(o_ref.dtype)

    def attend(tk):
        _round_robin([head_chain(h, tk) for h in range(n_heads)])

    half = nb // 2
    if half == 0:
        attend(t)
    else:
        @pl.when(qi < half)
        def _():
            attend(half * MOBA_BLOCK)

        @pl.when(qi >= half)
        def _():
            attend(t)


def moba_prompt(p_at, heads):
    b, t, _ = p_at.shape
    hd = ATT_HEAD_DIM
    tq = MOBA_BLOCK
    nbp = V7X_LANES
    assert t % MOBA_BLOCK == 0 and t // MOBA_BLOCK <= nbp
    hps = 2 if heads % 2 == 0 else 1
    groups = heads // hps
    w = hps * hd
    return pl.pallas_call(
        _moba_prompt_kernel,
        grid=(b, groups, t // tq),
        in_specs=[pl.BlockSpec((1, tq, w), lambda i, h, j: (i, j, h)),
                  pl.BlockSpec((1, t, w), lambda i, h, j: (i, 0, groups + h)),
                  pl.BlockSpec((1, t, w), lambda i, h, j: (i, 0, 2 * groups + h))],
        out_specs=pl.BlockSpec((1, tq, w), lambda i, h, j: (i, j, h)),
        out_shape=jax.ShapeDtypeStruct((b, t, heads * hd), BF16),
        scratch_shapes=[pltpu.VMEM((hps, nbp, hd), F32), pltpu.VMEM((t, w), BF16), pltpu.VMEM((t, w), BF16)],
        compiler_params=_cparams("parallel", "parallel", "arbitrary"),
        name="moba_prompt",
    )(p_at, p_at, p_at)


def _kblock_sum_kernel(pt_ref, *refs, heads):
    *k_refs, o_ref = refs
    acc = None
    for k_ref in k_refs:
        x = k_ref[0, 0]
        s = jnp.sum(x.reshape(x.shape[0] // heads, heads, x.shape[1]), axis=0)
        acc = s if acc is None else acc + s
    o_ref[0, 0] = acc


def block_key_sums(ck, layer, page_table, heads, ppb):
    bs, n_pages = page_table.shape
    rows, hd = ck.shape[2:]
    nblk = n_pages // ppb
    page_spec = lambda k: pl.BlockSpec((1, 1, rows, hd), lambda i, j, pt: (layer, pt[i, j * ppb + k], 0, 0))
    return pl.pallas_call(
        functools.partial(_kblock_sum_kernel, heads=heads),
        grid_spec=pltpu.PrefetchScalarGridSpec(
            num_scalar_prefetch=1,
            grid=(bs, nblk),
            in_specs=[page_spec(k) for k in range(ppb)],
            out_specs=pl.BlockSpec((1, 1, heads, hd), lambda i, j, pt: (i, j, 0, 0)),
        ),
        out_shape=jax.ShapeDtypeStruct((bs, nblk, heads, hd), F32),
        compiler_params=_cparams("parallel", "arbitrary"),
        name="block_key_sums",
    )(page_table, *([ck] * ppb))


def _moba_sample_kernel(pt_ref, q_ref, km_ref, kn_ref, vn_ref, *refs, heads, t_new, pages_per_block, pages_per_step):
    k_refs = refs[:pages_per_step]
    v_refs = refs[pages_per_step:2 * pages_per_step]
    o_ref, m_ref, l_ref, acc_ref, sel_ref, bias_ref = refs[2 * pages_per_step:]
    j = pl.program_id(1)
    scale = ATT_HEAD_DIM ** -0.5
    qb = q_ref[0].astype(BF16)
    n_rows = qb.shape[0]

    def head_match(n_lanes):
        row_head = lax.broadcasted_iota(jnp.int32, (n_rows, n_lanes), 0) // t_new
        lane_head = lax.broadcasted_iota(jnp.int32, (n_rows, n_lanes), 1) % heads
        return row_head == lane_head

    @pl.when(j == 0)
    def _():
        km = km_ref[0].astype(BF16)
        g = _dg(qb, km, _NT)
        match = head_match(g.shape[1])
        picked = _select_topk_lanes(jnp.where(match, g, NEG_INF), match).astype(BF16)
        nblk = sel_ref.shape[1]
        fold = (lax.broadcasted_iota(jnp.int32, (g.shape[1], nblk), 0) // heads
                == lax.broadcasted_iota(jnp.int32, (g.shape[1], nblk), 1)).astype(BF16)
        sel_ref[...] = _dg(picked, fold, _NN)
        s = _dg(qb, kn_ref[0].astype(BF16), _NT) * scale
        row_t = lax.broadcasted_iota(jnp.int32, s.shape, 0) % t_new
        lane_t = lax.broadcasted_iota(jnp.int32, s.shape, 1) // heads
        s = jnp.where(head_match(s.shape[1]) & (lane_t <= row_t), s, NEG_INF)
        m0 = jnp.max(s, axis=1, keepdims=True)
        p = jnp.exp(s - m0)
        m_ref[...] = m0
        l_ref[...] = jnp.sum(p, axis=1, keepdims=True)
        acc_ref[...] = _dg(p.astype(BF16), vn_ref[0].astype(BF16), _NN)
        bias_ref[...] = jnp.where(head_match(bias_ref.shape[1]), 0.0, NEG_INF)

    lane = lax.broadcasted_iota(jnp.int32, sel_ref.shape, 1)
    sel = sel_ref[...]
    bias = bias_ref[...]
    scores = []
    m_new = m_ref[...]
    for k, k_ref in enumerate(k_refs):
        blk = (j * pages_per_step + k) // pages_per_block
        ok = jnp.sum(jnp.where(lane == blk, sel, 0.0), axis=1, keepdims=True) > 0.5
        s = _dg(qb, k_ref[0, 0].astype(BF16), _NT) * scale + bias
        s = jnp.where(ok, s, NEG_INF)
        scores.append(s)
        m_new = jnp.maximum(m_new, jnp.max(s, axis=1, keepdims=True))
    alpha = jnp.exp(m_ref[...] - m_new)
    l_new = alpha * l_ref[...]
    acc = alpha * acc_ref[...]
    for s, v_ref in zip(scores, v_refs):
        p = jnp.exp(s - m_new)
        l_new = l_new + jnp.sum(p, axis=1, keepdims=True)
        acc = acc + _dg(p.astype(BF16), v_ref[0, 0].astype(BF16), _NN)
    l_ref[...] = l_new
    acc_ref[...] = acc
    m_ref[...] = m_new

    @pl.when(j == pl.num_programs(1) - 1)
    def _():
        o_ref[0] = (acc_ref[...] / l_ref[...]).astype(o_ref.dtype)


def moba_sample(p_at, cache_k, cache_v, layer, page_table, heads):
    b, t, _ = p_at.shape
    hd = ATT_HEAD_DIM
    w = heads * hd
    depth, n_pool, page = cache_k.shape[:3]
    n_pages = page_table.shape[1]
    assert MOBA_BLOCK % page == 0 and (n_pages * page) % MOBA_BLOCK == 0 and t <= MOBA_BLOCK
    assert n_pages * page // MOBA_BLOCK >= MOBA_TOPK
    ppb = MOBA_BLOCK // page
    nblk = n_pages // ppb
    ck = cache_k.reshape(depth, n_pool, page * heads, hd)
    cv = cache_v.reshape(depth, n_pool, page * heads, hd)
    ksum = block_key_sums(ck, layer, page_table, heads, ppb)
    kmean = (ksum * (1.0 / MOBA_BLOCK)).reshape(b, nblk * heads, hd)
    q = p_at[..., :w].reshape(b, t, heads, hd).transpose(0, 2, 1, 3).reshape(b, heads * t, hd)
    kn = p_at[..., w:2 * w].reshape(b, t * heads, hd)
    vn = p_at[..., 2 * w:].reshape(b, t * heads, hd)
    rows = heads * t
    pps = 4 if n_pages % 4 == 0 else 1
    kernel = functools.partial(_moba_sample_kernel, heads=heads, t_new=t, pages_per_block=ppb, pages_per_step=pps)
    per_b = lambda r: pl.BlockSpec((1, r, hd), lambda i, j, pt: (i, 0, 0))
    page_spec = lambda k: pl.BlockSpec((1, 1, page * heads, hd), lambda i, j, pt: (layer, pt[i, j * pps + k], 0, 0))
    o = pl.pallas_call(
        kernel,
        grid_spec=pltpu.PrefetchScalarGridSpec(
            num_scalar_prefetch=1,
            grid=(b, n_pages // pps),
            in_specs=[per_b(rows), per_b(nblk * heads), per_b(t * heads), per_b(t * heads)]
            + [page_spec(k) for k in range(pps)] * 2,
            out_specs=per_b(rows),
            scratch_shapes=[pltpu.VMEM((rows, 1), F32), pltpu.VMEM((rows, 1), F32),
                            pltpu.VMEM((rows, hd), F32), pltpu.VMEM((rows, nblk), F32),
                            pltpu.VMEM((rows, page * heads), F32)],
        ),
        out_shape=jax.ShapeDtypeStruct((b, rows, hd), BF16),
        compiler_params=_cparams("parallel", "arbitrary"),
        name="moba_sample",
    )(page_table, q, kmean, kn, vn, *([ck] * pps), *([cv] * pps))
    return o.reshape(b, heads, t, hd).transpose(0, 2, 1, 3).reshape(b, t, w)


def _ln_rows(y, g, b):
    mu = jnp.mean(y, axis=-1, keepdims=True)
    d = y - mu
    var = jnp.mean(d * d, axis=-1, keepdims=True)
    return d * lax.rsqrt(var + LN_EPS) * g + b


def _resid_ln_mod_kernel(x_ref, f_ref, gt_ref, g_ref, b_ref, sc_ref, sh_ref, rw_ref, x1_ref, h_ref, lg_ref):
    x1 = _ln_rows(DEEPNORM_ALPHA * x_ref[0] + gt_ref[0] * f_ref[0], g_ref[...], b_ref[...])
    x1_ref[0] = x1
    h = x1 * (1.0 + sc_ref[0]) + sh_ref[0]
    hb = h.astype(h_ref.dtype)
    h_ref[0] = hb
    lg_ref[0] = _dg(hb, rw_ref[...].astype(BF16), _NN)


def resid_ln_mod(x, f, gt, g, b, sc, sh, router_w):
    bsz, t, d = x.shape
    ne = router_w.shape[1]
    tt = min(t, 256)
    tok = pl.BlockSpec((1, tt, d), lambda i, j: (i, j, 0))
    per_b = pl.BlockSpec((1, 1, d), lambda i, j: (i, 0, 0))
    vec = pl.BlockSpec((1, d), lambda i, j: (0, 0))
    return pl.pallas_call(
        _resid_ln_mod_kernel,
        grid=(bsz, t // tt),
        in_specs=[tok, tok, per_b, vec, vec, per_b, per_b, pl.BlockSpec((d, ne), lambda i, j: (0, 0))],
        out_specs=[tok, tok, pl.BlockSpec((1, tt, ne), lambda i, j: (i, j, 0))],
        out_shape=[jax.ShapeDtypeStruct((bsz, t, d), F32), jax.ShapeDtypeStruct((bsz, t, d), BF16),
                   jax.ShapeDtypeStruct((bsz, t, ne), F32)],
        compiler_params=_cparams("parallel", "parallel"),
        name="resid_ln_mod",
    )(x, f, gt, g.reshape(1, d), b.reshape(1, d), sc, sh, router_w)


def _resid_ln_kernel(x_ref, f_ref, gt_ref, g_ref, b_ref, o_ref):
    o_ref[0] = _ln_rows(DEEPNORM_ALPHA * x_ref[0] + gt_ref[0] * f_ref[0], g_ref[...], b_ref[...])


def resid_ln(x, f, gt, g, b):
    bsz, t, d = x.shape
    tt = min(t, 256)
    tok = pl.BlockSpec((1, tt, d), lambda i, j: (i, j, 0))
    per_b = pl.BlockSpec((1, 1, d), lambda i, j: (i, 0, 0))
    vec = pl.BlockSpec((1, d), lambda i, j: (0, 0))
    return pl.pallas_call(
        _resid_ln_kernel,
        grid=(bsz, t // tt),
        in_specs=[tok, tok, per_b, vec, vec],
        out_specs=tok,
        out_shape=jax.ShapeDtypeStruct((bsz, t, d), F32),
        compiler_params=_cparams("parallel", "parallel"),
        name="resid_ln",
    )(x, f, gt, g.reshape(1, d), b.reshape(1, d))


def _moe_up_kernel(be_ref, bf_ref, nu_ref, x_ref, w1_ref, w3_ref, o_ref, w1b_ref, w3b_ref):
    i = pl.program_id(1)

    @pl.when(i < nu_ref[0])
    def _():
        @pl.when(bf_ref[i] == 1)
        def _():
            w1b_ref[...] = w1_ref[0, 0].astype(BF16)
            w3b_ref[...] = w3_ref[0, 0].astype(BF16)

        x = x_ref[...]
        a = jnp.dot(x, w1b_ref[...], preferred_element_type=F32)
        b = jnp.dot(x, w3b_ref[...], preferred_element_type=F32)
        o_ref[...] = (_silu(a) * b).astype(o_ref.dtype)


def _moe_down_kernel(be_ref, bf_ref, nu_ref, h_ref, w2_ref, sg_ref, o_ref, w2b_ref):
    i = pl.program_id(1)

    @pl.when(i < nu_ref[0])
    def _():
        @pl.when(bf_ref[i] == 1)
        def _():
            w2b_ref[...] = w2_ref[0, 0].astype(BF16)

        o_ref[...] = jnp.dot(h_ref[...], w2b_ref[...], preferred_element_type=F32) * sg_ref[...]


def moe_experts(xs, row_gate, blk_e, blk_first, n_used, w1, w3, w2, layer, tf=512, tn=1024):
    cap, d = xs.shape
    ff = w1.shape[-1]
    nblk = cap // MOE_ROWS
    R = MOE_ROWS
    clamp = lambda i, nu: jnp.minimum(i, nu[0] - 1)
    hmid = pl.pallas_call(
        _moe_up_kernel,
        grid_spec=pltpu.PrefetchScalarGridSpec(
            num_scalar_prefetch=3,
            grid=(ff // tf, nblk),
            in_specs=[pl.BlockSpec((R, d), lambda f, i, be, bf, nu: (clamp(i, nu), 0)),
                      pl.BlockSpec((1, 1, d, tf), lambda f, i, be, bf, nu: (layer, be[clamp(i, nu)], 0, f)),
                      pl.BlockSpec((1, 1, d, tf), lambda f, i, be, bf, nu: (layer, be[clamp(i, nu)], 0, f))],
            out_specs=pl.BlockSpec((R, tf), lambda f, i, be, bf, nu: (clamp(i, nu), f)),
            scratch_shapes=[pltpu.VMEM((d, tf), BF16), pltpu.VMEM((d, tf), BF16)],
        ),
        out_shape=jax.ShapeDtypeStruct((cap, ff), BF16),
        compiler_params=_cparams("arbitrary", "arbitrary"),
        name="moe_up",
    )(blk_e, blk_first, n_used, xs, w1, w3)
    return pl.pallas_call(
        _moe_down_kernel,
        grid_spec=pltpu.PrefetchScalarGridSpec(
            num_scalar_prefetch=3,
            grid=(d // tn, nblk),
            in_specs=[pl.BlockSpec((R, ff), lambda n, i, be, bf, nu: (clamp(i, nu), 0)),
                      pl.BlockSpec((1, 1, ff, tn), lambda n, i, be, bf, nu: (layer, be[clamp(i, nu)], 0, n)),
                      pl.BlockSpec((R, 1), lambda n, i, be, bf, nu: (clamp(i, nu), 0))],
            out_specs=pl.BlockSpec((R, tn), lambda n, i, be, bf, nu: (clamp(i, nu), n)),
            scratch_shapes=[pltpu.VMEM((ff, tn), BF16)],
        ),
        out_shape=jax.ShapeDtypeStruct((cap, d), F32),
        compiler_params=_cparams("arbitrary", "arbitrary"),
        name="moe_down",
    )(blk_e, blk_first, n_used, hmid, w2, row_gate)


def moe_ffn(h, logits, router_b, w1, w3, w2, layer):
    n_tok, d = h.shape
    probs = jax.nn.softmax(logits, axis=-1)
    sel = (probs + router_b).reshape(n_tok, N_EXPERT_GROUPS, EXPERTS_PER_GROUP)
    assert TOP_K == 2
    first = jnp.argmax(sel, axis=-1)
    rest = jnp.where(jnp.arange(EXPERTS_PER_GROUP) == first[..., None], NEG_INF, sel)
    grp_score = jnp.max(sel, axis=-1) + jnp.max(rest, axis=-1)
    best = jnp.argmax(grp_score, axis=-1)
    in_grp = jnp.take_along_axis(sel, best[:, None, None], axis=1)[:, 0]
    _, local = lax.top_k(in_grp, TOP_K)
    eidx = best[:, None] * EXPERTS_PER_GROUP + local
    gates = jnp.take_along_axis(probs, eidx, axis=1)
    gates = gates / jnp.sum(gates, axis=-1, keepdims=True)

    n_asg = n_tok * TOP_K
    flat_e = eidx.reshape(n_asg).astype(jnp.int32)
    flat_g = gates.reshape(n_asg)
    order = jnp.argsort(flat_e).astype(jnp.int32)
    se = flat_e[order]
    counts = jnp.sum((flat_e[:, None] == jnp.arange(N_EXPERTS, dtype=jnp.int32)[None, :]).astype(jnp.int32), axis=0)
    starts = jnp.cumsum(counts) - counts
    pcounts = (counts + MOE_ROWS - 1) // MOE_ROWS * MOE_ROWS
    pends = jnp.cumsum(pcounts)
    pstarts = pends - pcounts
    dest = (pstarts[se] + jnp.arange(n_asg, dtype=jnp.int32) - starts[se]).astype(jnp.int32)
    n_blk = -(-(n_asg + N_EXPERTS * (MOE_ROWS - 1)) // MOE_ROWS)
    cap = n_blk * MOE_ROWS
    blk_start = jnp.arange(n_blk, dtype=jnp.int32) * MOE_ROWS
    blk_e = jnp.minimum(jnp.sum(pends[None, :] <= blk_start[:, None], axis=1), N_EXPERTS - 1).astype(jnp.int32)
    slot_e = jnp.repeat(blk_e, MOE_ROWS)
    rank = jnp.arange(cap, dtype=jnp.int32) - pstarts[slot_e]
    filled = rank < counts[slot_e]
    src = order[jnp.where(filled, starts[slot_e] + rank, 0)]
    buf_tok = jnp.where(filled, src // TOP_K, 0).astype(jnp.int32)
    buf_gate = jnp.where(filled, flat_g[src], 0.0)
    pos = dest[jnp.argsort(order)]
    blk_first = jnp.concatenate([jnp.ones((1,), jnp.int32), (blk_e[1:] != blk_e[:-1]).astype(jnp.int32)])
    n_used = (pends[-1:] // MOE_ROWS).astype(jnp.int32)

    xs = h[buf_tok]
    ys = moe_experts(xs, buf_gate[:, None], blk_e, blk_first, n_used, w1, w3, w2, layer)
    pos = pos.reshape(n_tok, TOP_K)
    return ys[pos[:, 0]] + ys[pos[:, 1]]


def _mixing_sublayer(x, ada, shift0, wkv0, conv0, kv_past, lp, wts, router_w):
    b, t, d = x.shape
    sh1, sc1, gt1, sh2, sc2 = [ada[:, i:i + 1] for i in range(5)]
    h = modulate(x, sc1, sh1).reshape(b * t, d)
    width = d // 4
    heads = (d - 2 * width) // ATT_HEAD_DIM
    p_rw = matmul(h, wts['w_rw']).reshape(b, t, -1)
    p_cv = matmul(h, wts['w_cv']).reshape(b, t, -1)
    p_at = matmul(h, wts['w_at']).reshape(b, t, -1)
    o_rw, wkv_new, shift_new = rwkv7_group(p_rw, shift0, wkv0, lp, width)
    o_cv, conv_new = conformer_conv_group(p_cv, conv0, lp)
    if kv_past is None:
        o_at = moba_prompt(p_at, heads)
    else:
        o_at = moba_sample(p_at, *kv_past, heads)
    aw = heads * ATT_HEAD_DIM
    k_new = p_at[..., aw:2 * aw].reshape(b, t, heads, ATT_HEAD_DIM)
    v_new = p_at[..., 2 * aw:].reshape(b, t, heads, ATT_HEAD_DIM)
    mixed = matmul(jnp.concatenate([o_rw, o_cv, o_at], axis=-1).reshape(b * t, d), wts['w_out']).reshape(b, t, d)
    x1, h2, logits = resid_ln_mod(x, mixed, gt1, lp['ln1_g'], lp['ln1_b'], sc2, sh2, router_w)
    return x1, h2, logits, (k_new, v_new, wkv_new, shift_new, conv_new)


def kernel(x_prompt, x_sample, cache_k, cache_v, page_table, state_wkv, state_shift, state_conv,
           c_prompt, c_sample, w_ada, b_ada, w_in, w_out, rw_mix, rw_w0, rw_w2, rw_a0, rw_a2, rw_g2,
           rw_kk, rw_ka, rw_rk, rw_lnx_g, rw_lnx_b, cv_dw, cv_db, cv_ln_g, cv_ln_b,
           ln1_g, ln1_b, ln2_g, ln2_b, router_w, router_b, ex_w1, ex_w3, ex_w2):
    bp, tp, d = x_prompt.shape
    bs, ts, _ = x_sample.shape
    depth = w_in.shape[0]
    width = d // 4
    rw_cols = 3 * width + RW_DECAY_LORA + RW_ICLR_LORA + RW_GATE_LORA
    rw_pad = -(-rw_cols // 512) * 512
    at0 = rw_cols + 2 * width
    heads_rw = width // RW_HEAD_DIM

    c_all = jnp.concatenate([c_prompt, c_sample], axis=0)
    c_rows = -(-c_all.shape[0] // 8) * 8
    c_all = jnp.pad(c_all, ((0, c_rows - c_all.shape[0]), (0, 0)))

    yp, ys = x_prompt, x_sample
    st_p, st_s = [], []
    for l in range(depth):
        lp = {'rw_mix': rw_mix[l], 'rw_w0': rw_w0[l], 'rw_w2': rw_w2[l], 'rw_a0': rw_a0[l], 'rw_a2': rw_a2[l],
              'rw_g2': rw_g2[l], 'rw_kk': rw_kk[l], 'rw_ka': rw_ka[l], 'rw_rk': rw_rk[l],
              'rw_lnx_g': rw_lnx_g[l], 'rw_lnx_b': rw_lnx_b[l],
              'cv_dw': cv_dw[l], 'cv_db': cv_db[l], 'cv_ln_g': cv_ln_g[l], 'cv_ln_b': cv_ln_b[l],
              'ln1_g': ln1_g[l], 'ln1_b': ln1_b[l]}
        wl = w_in[l]
        wts = {'w_rw': jnp.pad(wl[:, :rw_cols].astype(BF16), ((0, 0), (0, rw_pad - rw_cols))),
               'w_cv': wl[:, rw_cols:at0].astype(BF16),
               'w_at': wl[:, at0:].astype(BF16),
               'w_out': w_out[l].astype(BF16)}
        ada = ada_proj(c_all, w_ada, b_ada, l).reshape(c_rows, 6, d)
        ada_p, ada_s = ada[:bp], ada[bp:bp + bs]

        zeros_p = (jnp.zeros((bp, rw_cols), F32), jnp.zeros((bp, heads_rw, RW_HEAD_DIM, RW_HEAD_DIM), F32),
                   jnp.zeros((bp, CONV_K - 1, width), F32))
        xp1, hp2, lgp, sp = _mixing_sublayer(yp, ada_p, *zeros_p, None, lp, wts, router_w)
        xs1, hs2, lgs, ss = _mixing_sublayer(ys, ada_s, state_shift[l], state_wkv[l], state_conv[l],
                                             (cache_k, cache_v, l, page_table), lp, wts, router_w)
        h_all = jnp.concatenate([hp2.reshape(bp * tp, d), hs2.reshape(bs * ts, d)], axis=0)
        lg_all = jnp.concatenate([lgp.reshape(bp * tp, -1), lgs.reshape(bs * ts, -1)], axis=0)
        f_all = moe_ffn(h_all, lg_all, router_b, ex_w1, ex_w3, ex_w2, l)
        yp = resid_ln(xp1, f_all[:bp * tp].reshape(bp, tp, d), ada_p[:, 5:6], ln2_g[l], ln2_b[l])
        ys = resid_ln(xs1, f_all[bp * tp:].reshape(bs, ts, d), ada_s[:, 5:6], ln2_g[l], ln2_b[l])
        st_p.append(sp)
        st_s.append(ss)
    k_p, v_p, wkv_p, shift_p, conv_p = [jnp.stack([s[i] for s in st_p]) for i in range(5)]
    k_s, v_s, wkv_s, shift_s, conv_s = [jnp.stack([s[i] for s in st_s]) for i in range(5)]
    return (yp, ys, k_p, v_p, wkv_p, shift_p, conv_p, k_s, v_s, wkv_s, shift_s, conv_s)
```
